```python
import math
import numpy as np
import jax, jax.numpy as jnp
from jax import lax

D_MODEL = 1024
BATCH = 8
SEQ = 4096
DEPTH = 2

HEAD_DIM = 64
ATTN_Q_HEADS = 8
ATTN_KV_HEADS = 2
ATTN_GROUP = ATTN_Q_HEADS // ATTN_KV_HEADS
ATTN_WIDTH = ATTN_Q_HEADS * HEAD_DIM
KV_WIDTH = ATTN_KV_HEADS * HEAD_DIM
WINDOW = 128
BLOCK = 128
N_BUCKETS = 32
MAX_DISTANCE = 128
RW_HEADS = 8
RW_HEAD = 64
RW_WIDTH = RW_HEADS * RW_HEAD
DECAY_RANK = 64
A_RANK = 64
V_RANK = 32
G_RANK = 128
GN_EPS = 64e-5
D_FF = 2816
ALPHA = (2 * DEPTH) ** 0.25
BETA = (8 * DEPTH) ** -0.25
LN_EPS = 1e-5

OFF_GATE_A = 0
OFF_GATE_B = D_MODEL
OFF_Q = 2 * D_MODEL
OFF_K = OFF_Q + ATTN_WIDTH
OFF_V = OFF_K + KV_WIDTH
OFF_RW = OFF_V + KV_WIDTH
RW_COLS = 3 * RW_WIDTH + DECAY_RANK + A_RANK + G_RANK
PROJ_WIDTH = OFF_RW + RW_COLS

kernel_name = 'hybrid_swa_rwkv7_macaron_deepnorm'


def layer_norm(x, g, b, eps=LN_EPS):
    xf = x.astype(jnp.float32)
    mu = jnp.mean(xf, axis=-1, keepdims=True)
    var = jnp.mean(jnp.square(xf - mu), axis=-1, keepdims=True)
    return ((xf - mu) * lax.rsqrt(var + eps) * g + b).astype(x.dtype)


def swiglu(x, w_gu, w_down):
    gate, up = jnp.split(x @ w_gu, 2, axis=-1)
    return (jax.nn.silu(gate) * up) @ w_down


def t5_bucket(n):
    max_exact = N_BUCKETS // 2
    nf = jnp.maximum(n, 1).astype(jnp.float32)
    large = max_exact + (jnp.log(nf / max_exact) / math.log(MAX_DISTANCE / max_exact)
                         * (N_BUCKETS - max_exact)).astype(jnp.int32)
    large = jnp.minimum(large, N_BUCKETS - 1)
    return jnp.where(n < max_exact, n, large)


def sliding_window_attention(q, k, v, dist_bias, sinks):
    b_, s_, _ = q.shape
    nb = s_ // BLOCK
    qb = q.reshape(b_, nb, BLOCK, ATTN_KV_HEADS, ATTN_GROUP, HEAD_DIM)

    def band(t):
        tb = t.reshape(b_, nb, BLOCK, ATTN_KV_HEADS, HEAD_DIM)
        prev = jnp.pad(tb, ((0, 0), (1, 0), (0, 0), (0, 0), (0, 0)))[:, :-1]
        return jnp.concatenate([prev, tb], axis=2)

    kb, vb = band(k), band(v)
    s = jnp.einsum('bnqhgd,bnkhd->bnhgqk', qb, kb).astype(jnp.float32) * (HEAD_DIM ** -0.5)
    qi = np.arange(BLOCK)[:, None]
    kj = np.arange(2 * BLOCK)[None, :]
    dist = qi + BLOCK - kj
    local = (dist >= 0) & (dist < WINDOW)
    first = (np.arange(nb)[:, None, None] > 0) | (kj[None] >= BLOCK)
    mask = local[None] & first
    bias = dist_bias[:, np.clip(dist, 0, WINDOW - 1)].astype(jnp.float32)
    bias = bias.reshape(ATTN_KV_HEADS, ATTN_GROUP, BLOCK, 2 * BLOCK)
    s = jnp.where(mask[None, :, None, None], s + bias, -jnp.inf)
    sink = sinks.astype(jnp.float32).reshape(ATTN_KV_HEADS, ATTN_GROUP, 1, 1)
    m = jnp.maximum(jnp.max(s, axis=-1, keepdims=True), sink)
    p = jnp.exp(s - m)
    p = p / (jnp.sum(p, axis=-1, keepdims=True) + jnp.exp(sink - m))
    o = jnp.einsum('bnhgqk,bnkhd->bnqhgd', p.astype(v.dtype), vb)
    return o.reshape(b_, s_, ATTN_WIDTH)


def wkv7_scan(r, decay, k, v, a, b):
    def step(state, inp):
        r_t, w_t, k_t, v_t, a_t, b_t = inp
        sa = jnp.einsum('bhij,bhj->bhi', state, a_t)
        state = (state * w_t[:, :, None, :] + sa[..., None] * b_t[:, :, None, :]
                 + v_t[..., None] * k_t[:, :, None, :])
        return state, jnp.einsum('bhij,bhj->bhi', state, r_t)

    xs = tuple(jnp.moveaxis(t.astype(jnp.float32), 1, 0) for t in (r, decay, k, v, a, b))
    b_, _, h_, n_ = r.shape
    s0 = jnp.zeros((b_, h_, n_, n_), jnp.float32)
    _, y = lax.scan(step, s0, xs)
    return jnp.moveaxis(y, 0, 1)


def rwkv7_mix(u, w0, w2, a0, a2, g2, k_k, k_a, r_k, gn_g, gn_b, v_first, vmix):
    b_, s_, _ = u.shape
    c = RW_WIDTH
    r = u[..., :c]
    k = u[..., c:2 * c]
    v = u[..., 2 * c:3 * c]
    o = 3 * c
    xw = u[..., o:o + DECAY_RANK]
    xa = u[..., o + DECAY_RANK:o + DECAY_RANK + A_RANK]
    xg = u[..., o + DECAY_RANK + A_RANK:]
    w = -jax.nn.softplus(-(w0 + jnp.tanh(xw) @ w2)) - 0.5
    decay = jnp.exp(-jnp.exp(w.astype(jnp.float32)))
    a = jax.nn.sigmoid(a0 + xa @ a2)
    g = jax.nn.sigmoid(xg) @ g2
    if vmix is None:
        v_first = v
    else:
        v0, v1, v2 = vmix
        v = v + (v_first - v) * jax.nn.sigmoid(v0 + (v @ v1) @ v2)
    heads = lambda t: t.reshape(b_, s_, RW_HEADS, RW_HEAD)
    kk = heads(k * k_k).astype(jnp.float32)
    kk = kk / jnp.maximum(jnp.sqrt(jnp.sum(kk * kk, axis=-1, keepdims=True)), 1e-12)
    k = k * (1.0 + (a - 1.0) * k_a)
    rh, kh, vh, ah = heads(r), heads(k), heads(v), heads(a)
    y = wkv7_scan(rh, heads(decay), kh, vh, -kk, kk * ah)
    mu = jnp.mean(y, axis=-1, keepdims=True)
    var = jnp.mean(jnp.square(y - mu), axis=-1, keepdims=True)
    y = ((y - mu) * lax.rsqrt(var + GN_EPS)).reshape(b_, s_, c) * gn_g + gn_b
    bonus = jnp.sum(rh * kh * r_k, axis=-1, keepdims=True) * vh
    y = y + bonus.reshape(b_, s_, c)
    return y * g, v_first


def token_mix(x, w_in, b_in, mu, sinks, dist_bias, w0, w2, a0, a2, g2, k_k, k_a, r_k,
              gn_g, gn_b, w_ba, w_bb, w_o, v_first, vmix):
    proj = x @ w_in + b_in
    gate_a = jax.nn.sigmoid(proj[..., OFF_GATE_A:OFF_GATE_B])
    gate_b = jax.nn.sigmoid(proj[..., OFF_GATE_B:OFF_Q])
    attn = sliding_window_attention(proj[..., OFF_Q:OFF_K], proj[..., OFF_K:OFF_V],
                                    proj[..., OFF_V:OFF_RW], dist_bias, sinks)
    u = proj[..., OFF_RW:]
    u_prev = jnp.pad(u, ((0, 0), (1, 0), (0, 0)))[:, :-1]
    u = u + (u_prev - u) * mu
    rw, v_first = rwkv7_mix(u, w0, w2, a0, a2, g2, k_k, k_a, r_k, gn_g, gn_b, v_first, vmix)
    merged = gate_a * (attn @ w_ba) + gate_b * (rw @ w_bb)
    return merged @ w_o, v_first


def setup_inputs(seed: int = 0) -> dict:
    key = jax.random.key(seed)
    ks = jax.random.split(key, 26)
    nrm = lambda k, shape, scale: jax.random.normal(k, shape, jnp.float32) * scale
    L = DEPTH
    c = RW_WIDTH
    return {
        'x': nrm(ks[0], (BATCH, SEQ, D_MODEL), 1.0),
        'ffn_w_gu': nrm(ks[1], (L, 2, D_MODEL, 2 * D_FF), D_MODEL ** -0.5),
        'ffn_w_down': nrm(ks[2], (L, 2, D_FF, D_MODEL), BETA * D_FF ** -0.5),
        'ln_g': 1.0 + nrm(ks[3], (L, 3, D_MODEL), 0.02),
        'ln_b': nrm(ks[4], (L, 3, D_MODEL), 0.02),
        'w_in': nrm(ks[5], (L, D_MODEL, PROJ_WIDTH), D_MODEL ** -0.5),
        'b_in': nrm(ks[6], (L, PROJ_WIDTH), 0.02),
        'rel_bias': nrm(ks[7], (N_BUCKETS, ATTN_Q_HEADS), 0.5),
        'attn_sinks': nrm(ks[8], (L, ATTN_Q_HEADS), 0.5),
        'shift_mu': jax.random.uniform(ks[9], (L, RW_COLS), jnp.float32),
        'rw_w0': jax.random.uniform(ks[10], (L, c), jnp.float32, -6.0, 1.0),
        'rw_w2': nrm(ks[11], (L, DECAY_RANK, c), 0.5 * DECAY_RANK ** -0.5),
        'rw_a0': nrm(ks[12], (L, c), 0.1),
        'rw_a2': nrm(ks[13], (L, A_RANK, c), 0.5 * A_RANK ** -0.5),
        'rw_g2': nrm(ks[14], (L, G_RANK, c), G_RANK ** -0.5),
        'rw_k_k': 0.85 + nrm(ks[15], (L, c), 0.02),
        'rw_k_a': 1.0 + nrm(ks[16], (L, c), 0.02),
        'rw_r_k': nrm(ks[17], (L, RW_HEADS, RW_HEAD), 0.1),
        'rw_gn_g': 1.0 + nrm(ks[18], (L, c), 0.02),
        'rw_gn_b': nrm(ks[19], (L, c), 0.02),
        'rw_v0': nrm(ks[20], (L - 1, c), 0.1),
        'rw_v1': nrm(ks[21], (L - 1, c, V_RANK), c ** -0.5),
        'rw_v2': nrm(ks[22], (L - 1, V_RANK, c), 0.5 * V_RANK ** -0.5),
        'w_branch_attn': nrm(ks[23], (L, ATTN_WIDTH, D_MODEL), BETA * ATTN_WIDTH ** -0.5),
        'w_branch_rwkv': nrm(ks[24], (L, c, D_MODEL), BETA * c ** -0.5),
        'w_out': nrm(ks[25], (L, D_MODEL, D_MODEL), BETA * D_MODEL ** -0.5),
    }


def reference(x, ffn_w_gu, ffn_w_down, ln_g, ln_b, w_in, b_in, rel_bias, attn_sinks,
              shift_mu, rw_w0, rw_w2, rw_a0, rw_a2, rw_g2, rw_k_k, rw_k_a, rw_r_k,
              rw_gn_g, rw_gn_b, rw_v0, rw_v1, rw_v2, w_branch_attn, w_branch_rwkv, w_out):
    dist_bias = rel_bias[t5_bucket(jnp.arange(WINDOW, dtype=jnp.int32))].T
    v_first = None
    for l in range(DEPTH):
        x = layer_norm(ALPHA * x + 0.5 * swiglu(x, ffn_w_gu[l, 0], ffn_w_down[l, 0]),
                       ln_g[l, 0], ln_b[l, 0])
        vmix = None if l == 0 else (rw_v0[l - 1], rw_v1[l - 1], rw_v2[l - 1])
        mix, v_first = token_mix(x, w_in[l], b_in[l], shift_mu[l], attn_sinks[l], dist_bias,
                                 rw_w0[l], rw_w2[l], rw_a0[l], rw_a2[l], rw_g2[l],
                                 rw_k_k[l], rw_k_a[l], rw_r_k[l], rw_gn_g[l], rw_gn_b[l],
                                 w_branch_attn[l], w_branch_rwkv[l], w_out[l], v_first, vmix)
        x = layer_norm(ALPHA * x + mix, ln_g[l, 1], ln_b[l, 1])
        x = layer_norm(ALPHA * x + 0.5 * swiglu(x, ffn_w_gu[l, 1], ffn_w_down[l, 1]),
                       ln_g[l, 2], ln_b[l, 2])
    return x
```

```python
import functools
import math

import jax
import jax.numpy as jnp
from jax import lax
from jax.experimental import pallas as pl
from jax.experimental.pallas import tpu as pltpu

F32 = jnp.float32
BF16 = jnp.bfloat16

D_MODEL = 1024
DEPTH = 2
HEAD_DIM = 64
ATTN_Q_HEADS = 8
ATTN_KV_HEADS = 2
ATTN_GROUP = ATTN_Q_HEADS // ATTN_KV_HEADS
ATTN_WIDTH = ATTN_Q_HEADS * HEAD_DIM
KV_WIDTH = ATTN_KV_HEADS * HEAD_DIM
WINDOW = 128
BLOCK = 128
N_BUCKETS = 32
MAX_DISTANCE = 128
RW_HEADS = 8
RW_HEAD = 64
RW_WIDTH = RW_HEADS * RW_HEAD
DECAY_RANK = 64
A_RANK = 64
V_RANK = 32
G_RANK = 128
GN_EPS = 64e-5
D_FF = 2816
ALPHA = (2 * DEPTH) ** 0.25
LN_EPS = 1e-5
RW_COLS = 3 * RW_WIDTH + DECAY_RANK + A_RANK + G_RANK

LANES = 128
MXU_DIM = 256
VMEM_LIMIT = 56 * 1024 * 1024

TM = 512
FF_CHUNK = MXU_DIM
N_FF_CHUNKS = D_FF // FF_CHUNK
TQ = 512
CHUNK = 64
WKV_TBLK = 256
PAIR = 2 * RW_HEAD
N_PAIRS = RW_WIDTH // PAIR

PW_GATES = 2 * D_MODEL
PW_Q = ATTN_WIDTH
PW_KD = 2 * KV_WIDTH
PW_VD = 2 * KV_WIDTH
PO_Q = PW_GATES
PO_KD = PO_Q + PW_Q
PO_VD = PO_KD + PW_KD
PO_RW = PO_VD + PW_VD
PROJ_COLS = PO_RW + RW_COLS


def _dot(a, b):
    return jnp.dot(a, b, preferred_element_type=F32)


def _dot_nt(a, b):
    return lax.dot_general(a, b, (((1,), (1,)), ((), ())), preferred_element_type=F32)


def _layer_norm(y, g, b):
    mu = jnp.mean(y, axis=-1, keepdims=True)
    d = y - mu
    var = jnp.mean(d * d, axis=-1, keepdims=True)
    return d * lax.rsqrt(var + LN_EPS) * g + b


def _resident(shape):
    nd = len(shape)
    return pl.BlockSpec(shape, lambda *_: (0,) * nd, pipeline_mode=pl.Buffered(1))


def _params(*sem):
    return pltpu.CompilerParams(dimension_semantics=sem, vmem_limit_bytes=VMEM_LIMIT)


def _ffn_ln_body(x_ref, wgu_ref, wd_ref, g_ref, b_ref, o_ref):
    x = x_ref[...]
    xb = x.astype(BF16)
    acc = jnp.zeros(x.shape, F32)
    for j in range(N_FF_CHUNKS):
        gu = _dot(xb, wgu_ref[j])
        h = jax.nn.silu(gu[:, :FF_CHUNK]) * gu[:, FF_CHUNK:]
        acc = acc + _dot(h.astype(BF16), wd_ref[j])
    o_ref[...] = _layer_norm(ALPHA * x + 0.5 * acc, g_ref[...], b_ref[...])


def _ffn_ln(x, wgu, wd, g, b):
    m = x.shape[0]
    return pl.pallas_call(
        _ffn_ln_body,
        grid=(m // TM,),
        in_specs=[pl.BlockSpec((TM, D_MODEL), lambda i: (i, 0)),
                  _resident(wgu.shape), _resident(wd.shape),
                  _resident(g.shape), _resident(b.shape)],
        out_specs=pl.BlockSpec((TM, D_MODEL), lambda i: (i, 0)),
        out_shape=jax.ShapeDtypeStruct((m, D_MODEL), F32),
        compiler_params=_params("parallel"),
        name="ffn_ln",
    )(x, wgu, wd, g, b)


def _softplus(z):
    return jnp.maximum(z, 0.0) + jnp.log1p(jnp.exp(-jnp.abs(z)))


def _proj_body(tiles_per_seq, has_vmix, *refs):
    (x_ref, w_ref, b_ref, mu_ref, w0_ref, w2_ref, a0_ref, a2_ref, g2_ref) = refs[:9]
    refs = refs[9:]
    if has_vmix:
        vf_ref, v0_ref, v1_ref, v2_ref = refs[:4]
        refs = refs[4:]
    (gab_ref, q_ref, kd_ref, vd_ref, r_ref, k_ref, v_ref, lw_ref, a_ref, g_ref, carry_ref) = refs

    i = pl.program_id(0)
    tm = x_ref.shape[0]
    proj = _dot(x_ref[...].astype(BF16), w_ref[...]) + b_ref[...]
    gab_ref[...] = jax.nn.sigmoid(proj[:, :PW_GATES]).astype(BF16)
    q_ref[...] = proj[:, PO_Q:PO_KD].astype(BF16)
    kd_ref[...] = proj[:, PO_KD:PO_VD].astype(BF16)
    vd_ref[...] = proj[:, PO_VD:PO_RW].astype(BF16)

    u = proj[:, PO_RW:]

    @pl.when(i % tiles_per_seq == 0)
    def _():
        carry_ref[...] = jnp.zeros(carry_ref.shape, F32)

    before_tile = carry_ref[7:8, :]
    carry_ref[...] = u[tm - 8:, :]
    row = lax.broadcasted_iota(jnp.int32, (tm, 1), 0)
    u_prev = jnp.where(row == 0, before_tile, pltpu.roll(u, 1, 0))
    u = u + (u_prev - u) * mu_ref[...]

    c = RW_WIDTH
    r = u[:, :c]
    k = u[:, c:2 * c]
    v = u[:, 2 * c:3 * c]
    o = 3 * c
    xw = u[:, o:o + DECAY_RANK]
    xa = u[:, o + DECAY_RANK:o + DECAY_RANK + A_RANK]
    xg = u[:, o + DECAY_RANK + A_RANK:]

    w = -_softplus(-(w0_ref[...] + _dot(jnp.tanh(xw).astype(BF16), w2_ref[...]))) - 0.5
    lw_ref[...] = -jnp.exp(w)
    a_ref[...] = jax.nn.sigmoid(a0_ref[...] + _dot(xa.astype(BF16), a2_ref[...]))
    g_ref[...] = _dot(jax.nn.sigmoid(xg).astype(BF16), g2_ref[...]).astype(BF16)
    if has_vmix:
        low = _dot(v.astype(BF16), v1_ref[...])
        mix = jax.nn.sigmoid(v0_ref[...] + _dot(low.astype(BF16), v2_ref[...]))
        v = v + (vf_ref[...] - v) * mix
    r_ref[...] = r
    k_ref[...] = k
    v_ref[...] = v


def _proj_prep(x, seq_len, w, b, mu, w0, w2, a0, a2, g2, vmix):
    m = x.shape[0]
    tile = lambda n: pl.BlockSpec((TM, n), lambda i: (i, 0))
    has_vmix = vmix is not None
    args = [x, w, b, mu, w0, w2, a0, a2, g2]
    in_specs = [tile(D_MODEL)] + [_resident(t.shape) for t in args[1:]]
    if has_vmix:
        vf, v0, v1, v2 = vmix
        args += [vf, v0, v1, v2]
        in_specs += [tile(RW_WIDTH), _resident(v0.shape), _resident(v1.shape), _resident(v2.shape)]
    widths = [(PW_GATES, BF16), (PW_Q, BF16), (PW_KD, BF16), (PW_VD, BF16)] \
        + [(RW_WIDTH, F32)] * 5 + [(RW_WIDTH, BF16)]
    return pl.pallas_call(
        functools.partial(_proj_body, seq_len // TM, has_vmix),
        grid=(m // TM,),
        in_specs=in_specs,
        out_specs=[tile(n) for n, _ in widths],
        out_shape=[jax.ShapeDtypeStruct((m, n), dt) for n, dt in widths],
        scratch_shapes=[pltpu.VMEM((8, RW_COLS), F32)],
        compiler_params=_params("arbitrary"),
        name="proj_prep",
    )(*args)


def _swa_body(q_ref, kc_ref, kp_ref, vc_ref, vp_ref, bias_ref, sink_ref, o_ref):
    i = pl.program_id(1)
    lane = lax.broadcasted_iota(jnp.int32, (1, LANES), 1)
    lo = lane < HEAD_DIM
    key = lax.broadcasted_iota(jnp.int32, (1, 2 * BLOCK), 1)
    no_prev = jnp.logical_and(i == 0, key < BLOCK)
    zero = jnp.zeros((), BF16)
    for j in range(TQ // BLOCK):
        rows = slice(j * BLOCK, (j + 1) * BLOCK)
        if j == 0:
            k_prev, v_prev = kp_ref[...], vp_ref[...]
        else:
            before = slice((j - 1) * BLOCK, j * BLOCK)
            k_prev, v_prev = kc_ref[before, :], vc_ref[before, :]
        k_band = jnp.concatenate([k_prev, kc_ref[rows, :]], axis=0)
        v_band = jnp.concatenate([v_prev, vc_ref[rows, :]], axis=0)
        for h in range(ATTN_KV_HEADS):
            cols = slice(h * LANES, (h + 1) * LANES)
            kb, vb = k_band[:, cols], v_band[:, cols]
            parts = []
            for p in (2 * h, 2 * h + 1):
                qp = q_ref[rows, p * LANES:(p + 1) * LANES]
                parts += [jnp.where(lo, qp, zero), jnp.where(lo, zero, qp)]
            s = _dot_nt(jnp.concatenate(parts, axis=0), kb) + bias_ref[h]
            if j == 0:
                s = jnp.where(no_prev, -jnp.inf, s)
            sink = sink_ref[h]
            mx = jnp.maximum(jnp.max(s, axis=-1, keepdims=True), sink)
            e = jnp.exp(s - mx)
            den = jnp.sum(e, axis=-1, keepdims=True) + jnp.exp(sink - mx)
            e = e.astype(BF16)
            v_lo, v_hi = jnp.where(lo, vb, zero), jnp.where(lo, zero, vb)
            for n, p in enumerate((2 * h, 2 * h + 1)):
                top = slice(2 * n * BLOCK, (2 * n + 1) * BLOCK)
                bot = slice((2 * n + 1) * BLOCK, (2 * n + 2) * BLOCK)
                out = _dot(e[top], v_lo) / den[top] + _dot(e[bot], v_hi) / den[bot]
                o_ref[rows, p * LANES:(p + 1) * LANES] = out.astype(BF16)


def _swa(q, kd, vd, bias, sink, batch, seq_len):
    q3 = q.reshape(batch, seq_len, PW_Q)
    kd3 = kd.reshape(batch, seq_len, PW_KD)
    vd3 = vd.reshape(batch, seq_len, PW_VD)
    cur = lambda n: pl.BlockSpec((None, TQ, n), lambda b, i: (b, i, 0))
    prev = lambda n: pl.BlockSpec(
        (None, BLOCK, n), lambda b, i: (b, jnp.maximum(i * (TQ // BLOCK) - 1, 0), 0))
    out = pl.pallas_call(
        _swa_body,
        grid=(batch, seq_len // TQ),
        in_specs=[cur(PW_Q), cur(PW_KD), prev(PW_KD), cur(PW_VD), prev(PW_VD),
                  _resident(bias.shape), _resident(sink.shape)],
        out_specs=cur(ATTN_WIDTH),
        out_shape=jax.ShapeDtypeStruct((batch, seq_len, ATTN_WIDTH), BF16),
        compiler_params=_params("parallel", "parallel"),
        name="swa",
    )(q3, kd3, kd3, vd3, vd3, bias, sink)
    return out.reshape(batch * seq_len, ATTN_WIDTH)


def _split3(x):
    h1 = x.astype(BF16)
    r1 = x - h1.astype(F32)
    h2 = r1.astype(BF16)
    h3 = (r1 - h2.astype(F32)).astype(BF16)
    return h1, h2, h3


def _wkv_body(r_ref, k_ref, v_ref, lw_ref, a_ref, g_ref,
              kk_ref, ka_ref, rk_ref, gng_ref, gnb_ref, o_ref, s_ref):
    t = pl.program_id(2)
    tblk = r_ref.shape[0]

    @pl.when(t == 0)
    def _():
        s_ref[...] = jnp.zeros(s_ref.shape, F32)

    lane = lax.broadcasted_iota(jnp.int32, (1, PAIR), 1)
    lo = lane < RW_HEAD
    ri = lax.broadcasted_iota(jnp.int32, (PAIR, PAIR), 0)
    ci = lax.broadcasted_iota(jnp.int32, (PAIR, PAIR), 1)
    same_head = (ri >= RW_HEAD) == (ci >= RW_HEAD)
    head_ones = jnp.where(same_head, 1.0, 0.0).astype(BF16)
    same_blk = (ri >= CHUNK) == (ci >= CHUNK)
    tok_r, tok_c = ri & (CHUNK - 1), ci & (CHUNK - 1)
    strict = jnp.logical_and(same_blk, tok_c < tok_r)
    incl = jnp.logical_and(same_blk, tok_c <= tok_r)
    eye = ri == ci
    rt_ = lax.broadcasted_iota(jnp.int32, (tblk, tblk), 0)
    ct_ = lax.broadcasted_iota(jnp.int32, (tblk, tblk), 1)
    chunk_tril = jnp.where(
        jnp.logical_and(rt_ // CHUNK == ct_ // CHUNK, ct_ <= rt_), 1.0, 0.0).astype(BF16)

    def seg_sum(x):
        return _dot(x.astype(BF16), head_ones)

    def stack(x):
        return jnp.concatenate([jnp.where(lo, x, 0.0), jnp.where(lo, 0.0, x)], axis=0)

    r, k, v, lw, a = r_ref[...], k_ref[...], v_ref[...], lw_ref[...], a_ref[...]
    kk = k * kk_ref[...]
    kk = kk / jnp.maximum(jnp.sqrt(seg_sum(kk * kk)), 1e-12)
    k = k * (1.0 + (a - 1.0) * ka_ref[...])
    b = kk * a
    bonus = seg_sum(r * k * rk_ref[...]) * v

    h1, h2, h3 = _split3(lw)
    cum = _dot(chunk_tril, h1) + _dot(chunk_tril, h2) + _dot(chunk_tril, h3)

    state = s_ref[...]
    ys = []
    for c in range(tblk // CHUNK):
        rows = slice(c * CHUNK, (c + 1) * CHUNK)
        cu = cum[rows]
        end = cu[CHUNK - 1:CHUNK, :]
        e_pos, e_neg = jnp.exp(cu), jnp.exp(-cu)
        e_prev, e_end = jnp.exp(cu - lw[rows]), jnp.exp(end - cu)
        at = stack(-kk[rows] * e_prev)
        rt = stack(r[rows] * e_pos)
        bt = stack(b[rows] * e_neg).astype(BF16)
        kt = stack(k[rows] * e_neg).astype(BF16)
        bh = stack(b[rows] * e_end)
        kh = stack(k[rows] * e_end)
        vs = stack(v[rows]).astype(BF16)

        lhs = jnp.concatenate([at, rt], axis=0).astype(BF16)
        gb, gk = _dot_nt(lhs, bt), _dot_nt(lhs, kt)
        a_ab = jnp.where(strict, gb[:PAIR], 0.0)
        a_rb = jnp.where(incl, gb[PAIR:], 0.0)
        a_kv = jnp.concatenate([jnp.where(strict, gk[:PAIR], 0.0),
                                jnp.where(incl, gk[PAIR:], 0.0)], axis=0)
        akv = _dot(a_kv.astype(BF16), vs)

        x = jnp.concatenate([at, akv[:PAIR]], axis=1)
        pw = a_ab.astype(BF16)
        n_levels = int(math.log2(CHUNK))
        for level in range(n_levels):
            x = x + _dot(pw, x.astype(BF16))
            if level + 1 < n_levels:
                pw = _dot(pw, pw).astype(BF16)
        xb = x.astype(BF16)

        qy = _dot(a_rb.astype(BF16), xb) + jnp.concatenate([rt, akv[PAIR:]], axis=1)
        bh_t = jnp.transpose(bh).astype(BF16)
        kh_t = jnp.transpose(kh).astype(BF16)
        decay_end = jnp.where(eye, jnp.exp(end), 0.0)
        mc = _dot(bh_t, xb) + jnp.concatenate([decay_end, _dot(kh_t, vs)], axis=1)

        sb = state.astype(BF16)
        y = _dot(qy[:, :PAIR].astype(BF16), sb) + qy[:, PAIR:]
        ys.append(y[:CHUNK] + y[CHUNK:])
        state = _dot(mc[:, :PAIR].astype(BF16), sb) + mc[:, PAIR:]
    s_ref[...] = state

    y = jnp.concatenate(ys, axis=0)
    yh, yl, _ = _split3(y)
    inv_n = 1.0 / RW_HEAD
    mu = (_dot(yh, head_ones) + _dot(yl, head_ones)) * inv_n
    d = y - mu
    var = seg_sum(d * d) * inv_n
    y = d * lax.rsqrt(var + GN_EPS) * gng_ref[...] + gnb_ref[...]
    o_ref[...] = ((y + bonus) * g_ref[...].astype(F32)).astype(BF16)


def _wkv7(r, k, v, lw, a, g, k_k, k_a, r_k, gn_g, gn_b, batch, seq_len):
    tok = pl.BlockSpec((None, WKV_TBLK, PAIR), lambda b, p, t: (b, t, p))
    par = pl.BlockSpec((1, PAIR), lambda b, p, t: (0, p))
    shape3 = (batch, seq_len, RW_WIDTH)
    out = pl.pallas_call(
        _wkv_body,
        grid=(batch, N_PAIRS, seq_len // WKV_TBLK),
        in_specs=[tok] * 6 + [par] * 5,
        out_specs=tok,
        out_shape=jax.ShapeDtypeStruct(shape3, BF16),
        scratch_shapes=[pltpu.VMEM((PAIR, PAIR), F32)],
        compiler_params=_params("parallel", "parallel", "arbitrary"),
        name="wkv7",
    )(*(t.reshape(shape3) for t in (r, k, v, lw, a, g)), k_k, k_a, r_k, gn_g, gn_b)
    return out.reshape(batch * seq_len, RW_WIDTH)


def _mix_ln_body(attn_ref, rw_ref, gab_ref, x_ref, wba_ref, wbb_ref, wo_ref, g_ref, b_ref, o_ref):
    gab = gab_ref[...]
    merged = (gab[:, :D_MODEL].astype(F32) * _dot(attn_ref[...], wba_ref[...])
              + gab[:, D_MODEL:].astype(F32) * _dot(rw_ref[...], wbb_ref[...]))
    mix = _dot(merged.astype(BF16), wo_ref[...])
    o_ref[...] = _layer_norm(ALPHA * x_ref[...] + mix, g_ref[...], b_ref[...])


def _mix_ln(attn, rw, gab, x, wba, wbb, wo, g, b):
    m = x.shape[0]
    tile = lambda n: pl.BlockSpec((TM, n), lambda i: (i, 0))
    return pl.pallas_call(
        _mix_ln_body,
        grid=(m // TM,),
        in_specs=[tile(ATTN_WIDTH), tile(RW_WIDTH), tile(PW_GATES), tile(D_MODEL)]
        + [_resident(t.shape) for t in (wba, wbb, wo, g, b)],
        out_specs=tile(D_MODEL),
        out_shape=jax.ShapeDtypeStruct((m, D_MODEL), F32),
        compiler_params=_params("parallel"),
        name="mix_ln",
    )(attn, rw, gab, x, wba, wbb, wo, g, b)


def _t5_bucket(n):
    max_exact = N_BUCKETS // 2
    nf = jnp.maximum(n, 1).astype(F32)
    large = max_exact + (jnp.log(nf / max_exact) / math.log(MAX_DISTANCE / max_exact)
                         * (N_BUCKETS - max_exact)).astype(jnp.int32)
    large = jnp.minimum(large, N_BUCKETS - 1)
    return jnp.where(n < max_exact, n, large)


def _attention_tables(rel_bias, sinks):
    dist_bias = rel_bias[_t5_bucket(jnp.arange(WINDOW, dtype=jnp.int32))].T
    qi = jnp.arange(BLOCK)[:, None]
    kj = jnp.arange(2 * BLOCK)[None, :]
    dist = qi + BLOCK - kj
    local = (dist >= 0) & (dist < WINDOW)
    bias = dist_bias[:, jnp.clip(dist, 0, WINDOW - 1)].astype(F32)
    bias = jnp.where(local[None], bias, -jnp.inf)
    bias = bias.reshape(ATTN_KV_HEADS, ATTN_GROUP * BLOCK, 2 * BLOCK)
    sink = jnp.repeat(sinks.astype(F32), BLOCK).reshape(ATTN_KV_HEADS, ATTN_GROUP * BLOCK, 1)
    return bias, sink


def _relayout_w_in(w_in, b_in):
    def cols(t):
        q = t[..., 2 * D_MODEL:2 * D_MODEL + ATTN_WIDTH] * (HEAD_DIM ** -0.5)
        k = t[..., 2 * D_MODEL + ATTN_WIDTH:2 * D_MODEL + ATTN_WIDTH + KV_WIDTH]
        v = t[..., 2 * D_MODEL + ATTN_WIDTH + KV_WIDTH:2 * D_MODEL + ATTN_WIDTH + 2 * KV_WIDTH]
        dup = lambda z: jnp.concatenate(
            [z[..., h * HEAD_DIM:(h + 1) * HEAD_DIM] for h in range(ATTN_KV_HEADS) for _ in (0, 1)],
            axis=-1)
        return jnp.concatenate([t[..., :2 * D_MODEL], q, dup(k), dup(v),
                                t[..., 2 * D_MODEL + ATTN_WIDTH + 2 * KV_WIDTH:]], axis=-1)
    return cols(w_in).astype(BF16), cols(b_in)[None, :]


def _relayout_ffn(w_gu, w_down):
    gate = w_gu[:, :D_FF].reshape(D_MODEL, N_FF_CHUNKS, FF_CHUNK)
    up = w_gu[:, D_FF:].reshape(D_MODEL, N_FF_CHUNKS, FF_CHUNK)
    wgu = jnp.concatenate([gate, up], axis=-1).transpose(1, 0, 2).astype(BF16)
    return wgu, w_down.reshape(N_FF_CHUNKS, FF_CHUNK, D_MODEL).astype(BF16)


def kernel(x, ffn_w_gu, ffn_w_down, ln_g, ln_b, w_in, b_in, rel_bias, attn_sinks, shift_mu,
           rw_w0, rw_w2, rw_a0, rw_a2, rw_g2, rw_k_k, rw_k_a, rw_r_k, rw_gn_g, rw_gn_b,
           rw_v0, rw_v1, rw_v2, w_branch_attn, w_branch_rwkv, w_out):
    batch, seq_len, _ = x.shape
    depth = w_in.shape[0]
    assert seq_len % TM == 0 and seq_len % TQ == 0 and seq_len % WKV_TBLK == 0
    row = lambda t: t.reshape(1, -1)
    h = x.reshape(batch * seq_len, D_MODEL)
    v_first = None
    for l in range(depth):
        h = _ffn_ln(h, *_relayout_ffn(ffn_w_gu[l, 0], ffn_w_down[l, 0]),
                    row(ln_g[l, 0]), row(ln_b[l, 0]))
        vmix = None if l == 0 else (v_first, row(rw_v0[l - 1]), rw_v1[l - 1].astype(BF16),
                                    rw_v2[l - 1].astype(BF16))
        w, b = _relayout_w_in(w_in[l], b_in[l])
        gab, q, kd, vd, r, k, v, lw, a, g = _proj_prep(
            h, seq_len, w, b, row(shift_mu[l]), row(rw_w0[l]), rw_w2[l].astype(BF16),
            row(rw_a0[l]), rw_a2[l].astype(BF16), rw_g2[l].astype(BF16), vmix)
        if l == 0:
            v_first = v
        bias, sink = _attention_tables(rel_bias, attn_sinks[l])
        attn = _swa(q, kd, vd, bias, sink, batch, seq_len)
        rw = _wkv7(r, k, v, lw, a, g, row(rw_k_k[l]), row(rw_k_a[l]), row(rw_r_k[l]),
                   row(rw_gn_g[l]), row(rw_gn_b[l]), batch, seq_len)
        h = _mix_ln(attn, rw, gab, h, w_branch_attn[l].astype(BF16),
                    w_branch_rwkv[l].astype(BF16), w_out[l].astype(BF16),
                    row(ln_g[l, 1]), row(ln_b[l, 1]))
        h = _ffn_ln(h, *_relayout_ffn(ffn_w_gu[l, 1], ffn_w_down[l, 1]),
                    row(ln_g[l, 2]), row(ln_b[l, 2]))
    return h.reshape(batch, seq_len, D_MODEL)
```

```python
import functools
import math

import jax
import jax.numpy as jnp
from jax import lax
from jax.experimental import pallas as pl
from jax.experimental.pallas import tpu as pltpu

F32 = jnp.float32
BF16 = jnp.bfloat16

D_MODEL = 1024
DEPTH = 2
HEAD_DIM = 64
ATTN_Q_HEADS = 8
ATTN_KV_HEADS = 2
ATTN_GROUP = ATTN_Q_HEADS // ATTN_KV_HEADS
ATTN_WIDTH = ATTN_Q_HEADS * HEAD_DIM
KV_WIDTH = ATTN_KV_HEADS * HEAD_DIM
WINDOW = 128
BLOCK = 128
N_BUCKETS = 32
MAX_DISTANCE = 128
RW_HEADS = 8
RW_HEAD = 64
RW_WIDTH = RW_HEADS * RW_HEAD
DECAY_RANK = 64
A_RANK = 64
V_RANK = 32
G_RANK = 128
GN_EPS = 64e-5
D_FF = 2816
ALPHA = (2 * DEPTH) ** 0.25
LN_EPS = 1e-5
RW_COLS = 3 * RW_WIDTH + DECAY_RANK + A_RANK + G_RANK

LANES = 128
MXU_DIM = 256
VMEM_LIMIT = 56 * 1024 * 1024

TM = 512
FF_CHUNK = MXU_DIM
N_FF_CHUNKS = D_FF // FF_CHUNK
TQ = 512
CHUNK = 64
WKV_TBLK = 256
PAIR = 2 * RW_HEAD
N_PAIRS = RW_WIDTH // PAIR

PW_GATES = 2 * D_MODEL
PW_Q = ATTN_WIDTH
PW_KD = 2 * KV_WIDTH
PW_VD = 2 * KV_WIDTH
PO_Q = PW_GATES
PO_KD = PO_Q + PW_Q
PO_VD = PO_KD + PW_KD
PO_RW = PO_VD + PW_VD
PROJ_COLS = PO_RW + RW_COLS


def _dot(a, b):
    return jnp.dot(a, b, preferred_element_type=F32)


def _dot_nt(a, b):
    return lax.dot_general(a, b, (((1,), (1,)), ((), ())), preferred_element_type=F32)


def _layer_norm(y, g, b):
    mu = jnp.mean(y, axis=-1, keepdims=True)
    d = y - mu
    var = jnp.mean(d * d, axis=-1, keepdims=True)
    return d * lax.rsqrt(var + LN_EPS) * g + b


def _resident(shape):
    nd = len(shape)
    return pl.BlockSpec(shape, lambda *_: (0,) * nd, pipeline_mode=pl.Buffered(1))


def _params(*sem):
    return pltpu.CompilerParams(dimension_semantics=sem, vmem_limit_bytes=VMEM_LIMIT)


def _ffn_ln_body(x_ref, wgu_ref, wd_ref, g_ref, b_ref, o_ref):
    x = x_ref[...]
    xb = x.astype(BF16)
    acc = jnp.zeros(x.shape, F32)
    for j in range(N_FF_CHUNKS):
        gu = _dot(xb, wgu_ref[j])
        h = jax.nn.silu(gu[:, :FF_CHUNK]) * gu[:, FF_CHUNK:]
        acc = acc + _dot(h.astype(BF16), wd_ref[j])
    o_ref[...] = _layer_norm(ALPHA * x + 0.5 * acc, g_ref[...], b_ref[...])


def _ffn_ln(x, wgu, wd, g, b):
    m = x.shape[0]
    return pl.pallas_call(
        _ffn_ln_body,
        grid=(m // TM,),
        in_specs=[pl.BlockSpec((TM, D_MODEL), lambda i: (i, 0)),
                  _resident(wgu.shape), _resident(wd.shape),
                  _resident(g.shape), _resident(b.shape)],
        out_specs=pl.BlockSpec((TM, D_MODEL), lambda i: (i, 0)),
        out_shape=jax.ShapeDtypeStruct((m, D_MODEL), F32),
        compiler_params=_params("parallel"),
        name="ffn_ln",
    )(x, wgu, wd, g, b)


def _softplus(z):
    return jnp.maximum(z, 0.0) + jnp.log1p(jnp.exp(-jnp.abs(z)))


def _proj_body(tiles_per_seq, has_vmix, *refs):
    (x_ref, w_ref, b_ref, mu_ref, w0_ref, w2_ref, a0_ref, a2_ref, g2_ref) = refs[:9]
    refs = refs[9:]
    if has_vmix:
        vf_ref, v0_ref, v1_ref, v2_ref = refs[:4]
        refs = refs[4:]
    (gab_ref, q_ref, kd_ref, vd_ref, r_ref, k_ref, v_ref, lw_ref, a_ref, g_ref, carry_ref) = refs

    i = pl.program_id(0)
    tm = x_ref.shape[0]
    proj = _dot(x_ref[...].astype(BF16), w_ref[...]) + b_ref[...]
    gab_ref[...] = jax.nn.sigmoid(proj[:, :PW_GATES]).astype(BF16)
    q_ref[...] = proj[:, PO_Q:PO_KD].astype(BF16)
    kd_ref[...] = proj[:, PO_KD:PO_VD].astype(BF16)
    vd_ref[...] = proj[:, PO_VD:PO_RW].astype(BF16)

    u = proj[:, PO_RW:]

    @pl.when(i % tiles_per_seq == 0)
    def _():
        carry_ref[...] = jnp.zeros(carry_ref.shape, F32)

    before_tile = carry_ref[7:8, :]
    carry_ref[...] = u[tm - 8:, :]
    row = lax.broadcasted_iota(jnp.int32, (tm, 1), 0)
    u_prev = jnp.where(row == 0, before_tile, pltpu.roll(u, 1, 0))
    u = u + (u_prev - u) * mu_ref[...]

    c = RW_WIDTH
    r = u[:, :c]
    k = u[:, c:2 * c]
    v = u[:, 2 * c:3 * c]
    o = 3 * c
    xw = u[:, o:o + DECAY_RANK]
    xa = u[:, o + DECAY_RANK:o + DECAY_RANK + A_RANK]
    xg = u[:, o + DECAY_RANK + A_RANK:]

    w = -_softplus(-(w0_ref[...] + _dot(jnp.tanh(xw).astype(BF16), w2_ref[...]))) - 0.5
    lw_ref[...] = -jnp.exp(w)
    a_ref[...] = jax.nn.sigmoid(a0_ref[...] + _dot(xa.astype(BF16), a2_ref[...]))
    g_ref[...] = _dot(jax.nn.sigmoid(xg).astype(BF16), g2_ref[...]).astype(BF16)
    if has_vmix:
        low = _dot(v.astype(BF16), v1_ref[...])
        mix = jax.nn.sigmoid(v0_ref[...] + _dot(low.astype(BF16), v2_ref[...]))
        v = v + (vf_ref[...] - v) * mix
    r_ref[...] = r
    k_ref[...] = k
    v_ref[...] = v


def _proj_prep(x, seq_len, w, b, mu, w0, w2, a0, a2, g2, vmix):
    m = x.shape[0]
    tile = lambda n: pl.BlockSpec((TM, n), lambda i: (i, 0))
    has_vmix = vmix is not None
    args = [x, w, b, mu, w0, w2, a0, a2, g2]
    in_specs = [tile(D_MODEL)] + [_resident(t.shape) for t in args[1:]]
    if has_vmix:
        vf, v0, v1, v2 = vmix
        args += [vf, v0, v1, v2]
        in_specs += [tile(RW_WIDTH), _resident(v0.shape), _resident(v1.shape), _resident(v2.shape)]
    widths = [(PW_GATES, BF16), (PW_Q, BF16), (PW_KD, BF16), (PW_VD, BF16)] \
        + [(RW_WIDTH, F32)] * 5 + [(RW_WIDTH, BF16)]
    return pl.pallas_call(
        functools.partial(_proj_body, seq_len // TM, has_vmix),
        grid=(m // TM,),
        in_specs=in_specs,
        out_specs=[tile(n) for n, _ in widths],
        out_shape=[jax.ShapeDtypeStruct((m, n), dt) for n, dt in widths],
        scratch_shapes=[pltpu.VMEM((8, RW_COLS), F32)],
        compiler_params=_params("arbitrary"),
        name="proj_prep",
    )(*args)


def _swa_body(q_ref, kc_ref, kp_ref, vc_ref, vp_ref, bias_ref, sink_ref, o_ref):
    i = pl.program_id(1)
    lane = lax.broadcasted_iota(jnp.int32, (1, LANES), 1)
    lo = lane < HEAD_DIM
    key = lax.broadcasted_iota(jnp.int32, (1, 2 * BLOCK), 1)
    no_prev = jnp.logical_and(i == 0, key < BLOCK)
    zero = jnp.zeros((), BF16)
    for j in range(TQ // BLOCK):
        rows = slice(j * BLOCK, (j + 1) * BLOCK)
        if j == 0:
            k_prev, v_prev = kp_ref[...], vp_ref[...]
        else:
            before = slice((j - 1) * BLOCK, j * BLOCK)
            k_prev, v_prev = kc_ref[before, :], vc_ref[before, :]
        k_band = jnp.concatenate([k_prev, kc_ref[rows, :]], axis=0)
        v_band = jnp.concatenate([v_prev, vc_ref[rows, :]], axis=0)
        for h in range(ATTN_KV_HEADS):
            cols = slice(h * LANES, (h + 1) * LANES)
            kb, vb = k_band[:, cols], v_band[:, cols]
            parts = []
            for p in (2 * h, 2 * h + 1):
                qp = q_ref[rows, p * LANES:(p + 1) * LANES]
                parts += [jnp.where(lo, qp, zero), jnp.where(lo, zero, qp)]
            s = _dot_nt(jnp.concatenate(parts, axis=0), kb) + bias_ref[h]
            if j == 0:
                s = jnp.where(no_prev, -jnp.inf, s)
            sink = sink_ref[h]
            mx = jnp.maximum(jnp.max(s, axis=-1, keepdims=True), sink)
            e = jnp.exp(s - mx)
            den = jnp.sum(e, axis=-1, keepdims=True) + jnp.exp(sink - mx)
            e = e.astype(BF16)
            v_lo, v_hi = jnp.where(lo, vb, zero), jnp.where(lo, zero, vb)
            for n, p in enumerate((2 * h, 2 * h + 1)):
                top = slice(2 * n * BLOCK, (2 * n + 1) * BLOCK)
                bot = slice((2 * n + 1) * BLOCK, (2 * n + 2) * BLOCK)
                out = _dot(e[top], v_lo) / den[top] + _dot(e[bot], v_hi) / den[bot]
                o_ref[rows, p * LANES:(p + 1) * LANES] = out.astype(BF16)


def _swa(q, kd, vd, bias, sink, batch, seq_len):
    q3 = q.reshape(batch, seq_len, PW_Q)
    kd3 = kd.reshape(batch, seq_len, PW_KD)
    vd3 = vd.reshape(batch, seq_len, PW_VD)
    cur = lambda n: pl.BlockSpec((None, TQ, n), lambda b, i: (b, i, 0))
    prev = lambda n: pl.BlockSpec(
        (None, BLOCK, n), lambda b, i: (b, jnp.maximum(i * (TQ // BLOCK) - 1, 0), 0))
    out = pl.pallas_call(
        _swa_body,
        grid=(batch, seq_len // TQ),
        in_specs=[cur(PW_Q), cur(PW_KD), prev(PW_KD), cur(PW_VD), prev(PW_VD),
                  _resident(bias.shape), _resident(sink.shape)],
        out_specs=cur(ATTN_WIDTH),
        out_shape=jax.ShapeDtypeStruct((batch, seq_len, ATTN_WIDTH), BF16),
        compiler_params=_params("parallel", "parallel"),
        name="swa",
    )(q3, kd3, kd3, vd3, vd3, bias, sink)
    return out.reshape(batch * seq_len, ATTN_WIDTH)


def _split3(x):
    h1 = x.astype(BF16)
    r1 = x - h1.astype(F32)
    h2 = r1.astype(BF16)
    h3 = (r1 - h2.astype(F32)).astype(BF16)
    return h1, h2, h3


def _wkv_body(r_ref, k_ref, v_ref, lw_ref, a_ref, g_ref,
              kk_ref, ka_ref, rk_ref, gng_ref, gnb_ref, o_ref, s_ref):
    t = pl.program_id(2)
    tblk = r_ref.shape[0]

    @pl.when(t == 0)
    def _():
        s_ref[...] = jnp.zeros(s_ref.shape, F32)

    lane = lax.broadcasted_iota(jnp.int32, (1, PAIR), 1)
    lo = lane < RW_HEAD
    ri = lax.broadcasted_iota(jnp.int32, (PAIR, PAIR), 0)
    ci = lax.broadcasted_iota(jnp.int32, (PAIR, PAIR), 1)
    same_head = (ri >= RW_HEAD) == (ci >= RW_HEAD)
    head_ones = jnp.where(same_head, 1.0, 0.0).astype(BF16)
    same_blk = (ri >= CHUNK) == (ci >= CHUNK)
    tok_r, tok_c = ri & (CHUNK - 1), ci & (CHUNK - 1)
    strict = jnp.logical_and(same_blk, tok_c < tok_r)
    incl = jnp.logical_and(same_blk, tok_c <= tok_r)
    eye = ri == ci
    rt_ = lax.broadcasted_iota(jnp.int32, (tblk, tblk), 0)
    ct_ = lax.broadcasted_iota(jnp.int32, (tblk, tblk), 1)
    chunk_tril = jnp.where(
        jnp.logical_and(rt_ // CHUNK == ct_ // CHUNK, ct_ <= rt_), 1.0, 0.0).astype(BF16)

    def seg_sum(x):
        return _dot(x.astype(BF16), head_ones)

    def stack(x):
        return jnp.concatenate([jnp.where(lo, x, 0.0), jnp.where(lo, 0.0, x)], axis=0)

    r, k, v, lw, a = r_ref[...], k_ref[...], v_ref[...], lw_ref[...], a_ref[...]
    kk = k * kk_ref[...]
    kk = kk / jnp.maximum(jnp.sqrt(seg_sum(kk * kk)), 1e-12)
    k = k * (1.0 + (a - 1.0) * ka_ref[...])
    b = kk * a
    bonus = seg_sum(r * k * rk_ref[...]) * v

    h1, h2, h3 = _split3(lw)
    cum = _dot(chunk_tril, h1) + _dot(chunk_tril, h2) + _dot(chunk_tril, h3)

    chunks = range(tblk // CHUNK)
    n_levels = int(math.log2(CHUNK))
    at, rt, bt, kt, bh, kh, vs, dec = [], [], [], [], [], [], [], []
    for c in chunks:
        rows = slice(c * CHUNK, (c + 1) * CHUNK)
        cu = cum[rows]
        end = cu[CHUNK - 1:CHUNK, :]
        e_pos, e_neg = jnp.exp(cu), jnp.exp(-cu)
        e_prev, e_end = jnp.exp(cu - lw[rows]), jnp.exp(end - cu)
        at.append(stack(-kk[rows] * e_prev))
        rt.append(stack(r[rows] * e_pos))
        bt.append(stack(b[rows] * e_neg).astype(BF16))
        kt.append(stack(k[rows] * e_neg).astype(BF16))
        bh.append(jnp.transpose(stack(b[rows] * e_end)).astype(BF16))
        kh.append(jnp.transpose(stack(k[rows] * e_end)).astype(BF16))
        vs.append(stack(v[rows]).astype(BF16))
        dec.append(jnp.where(eye, jnp.exp(end), 0.0))

    lhs = [jnp.concatenate([at[c], rt[c]], axis=0).astype(BF16) for c in chunks]
    gb = [_dot_nt(lhs[c], bt[c]) for c in chunks]
    gk = [_dot_nt(lhs[c], kt[c]) for c in chunks]
    a_kv = [jnp.concatenate([jnp.where(strict, gk[c][:PAIR], 0.0),
                             jnp.where(incl, gk[c][PAIR:], 0.0)], axis=0).astype(BF16) for c in chunks]
    akv = [_dot(a_kv[c], vs[c]) for c in chunks]
    khv = [_dot(kh[c], vs[c]) for c in chunks]
    a_rb = [jnp.where(incl, gb[c][PAIR:], 0.0).astype(BF16) for c in chunks]

    pw = [jnp.where(strict, gb[c][:PAIR], 0.0).astype(BF16) for c in chunks]
    x = [jnp.concatenate([at[c], akv[c][:PAIR]], axis=1) for c in chunks]
    for level in range(n_levels):
        x = [x[c] + _dot(pw[c], x[c].astype(BF16)) for c in chunks]
        if level + 1 < n_levels:
            pw = [_dot(pw[c], pw[c]).astype(BF16) for c in chunks]
    xb = [x[c].astype(BF16) for c in chunks]

    qy = [_dot(a_rb[c], xb[c]) + jnp.concatenate([rt[c], akv[c][PAIR:]], axis=1) for c in chunks]
    mc = [_dot(bh[c], xb[c]) + jnp.concatenate([dec[c], khv[c]], axis=1) for c in chunks]
    q_hat = [qy[c][:, :PAIR].astype(BF16) for c in chunks]
    m_hat = [mc[c][:, :PAIR].astype(BF16) for c in chunks]

    state = s_ref[...]
    ys = []
    for c in chunks:
        sb = state.astype(BF16)
        y = _dot(q_hat[c], sb) + qy[c][:, PAIR:]
        ys.append(y[:CHUNK] + y[CHUNK:])
        state = _dot(m_hat[c], sb) + mc[c][:, PAIR:]
    s_ref[...] = state

    y = jnp.concatenate(ys, axis=0)
    yh, yl, _ = _split3(y)
    inv_n = 1.0 / RW_HEAD
    mu = (_dot(yh, head_ones) + _dot(yl, head_ones)) * inv_n
    d = y - mu
    var = seg_sum(d * d) * inv_n
    y = d * lax.rsqrt(var + GN_EPS) * gng_ref[...] + gnb_ref[...]
    o_ref[...] = ((y + bonus) * g_ref[...].astype(F32)).astype(BF16)


def _wkv7(r, k, v, lw, a, g, k_k, k_a, r_k, gn_g, gn_b, batch, seq_len):
    tok = pl.BlockSpec((None, WKV_TBLK, PAIR), lambda b, p, t: (b, t, p))
    par = pl.BlockSpec((1, PAIR), lambda b, p, t: (0, p))
    shape3 = (batch, seq_len, RW_WIDTH)
    out = pl.pallas_call(
        _wkv_body,
        grid=(batch, N_PAIRS, seq_len // WKV_TBLK),
        in_specs=[tok] * 6 + [par] * 5,
        out_specs=tok,
        out_shape=jax.ShapeDtypeStruct(shape3, BF16),
        scratch_shapes=[pltpu.VMEM((PAIR, PAIR), F32)],
        compiler_params=_params("parallel", "parallel", "arbitrary"),
        name="wkv7",
    )(*(t.reshape(shape3) for t in (r, k, v, lw, a, g)), k_k, k_a, r_k, gn_g, gn_b)
    return out.reshape(batch * seq_len, RW_WIDTH)


def _mix_ln_body(attn_ref, rw_ref, gab_ref, x_ref, wba_ref, wbb_ref, wo_ref, g_ref, b_ref, o_ref):
    gab = gab_ref[...]
    merged = (gab[:, :D_MODEL].astype(F32) * _dot(attn_ref[...], wba_ref[...])
              + gab[:, D_MODEL:].astype(F32) * _dot(rw_ref[...], wbb_ref[...]))
    mix = _dot(merged.astype(BF16), wo_ref[...])
    o_ref[...] = _layer_norm(ALPHA * x_ref[...] + mix, g_ref[...], b_ref[...])


def _mix_ln(attn, rw, gab, x, wba, wbb, wo, g, b):
    m = x.shape[0]
    tile = lambda n: pl.BlockSpec((TM, n), lambda i: (i, 0))
    return pl.pallas_call(
        _mix_ln_body,
        grid=(m // TM,),
        in_specs=[tile(ATTN_WIDTH), tile(RW_WIDTH), tile(PW_GATES), tile(D_MODEL)]
        + [_resident(t.shape) for t in (wba, wbb, wo, g, b)],
        out_specs=tile(D_MODEL),
        out_shape=jax.ShapeDtypeStruct((m, D_MODEL), F32),
        compiler_params=_params("parallel"),
        name="mix_ln",
    )(attn, rw, gab, x, wba, wbb, wo, g, b)


def _t5_bucket(n):
    max_exact = N_BUCKETS // 2
    nf = jnp.maximum(n, 1).astype(F32)
    large = max_exact + (jnp.log(nf / max_exact) / math.log(MAX_DISTANCE / max_exact)
                         * (N_BUCKETS - max_exact)).astype(jnp.int32)
    large = jnp.minimum(large, N_BUCKETS - 1)
    return jnp.where(n < max_exact, n, large)


def _attention_tables(rel_bias, sinks):
    dist_bias = rel_bias[_t5_bucket(jnp.arange(WINDOW, dtype=jnp.int32))].T
    qi = jnp.arange(BLOCK)[:, None]
    kj = jnp.arange(2 * BLOCK)[None, :]
    dist = qi + BLOCK - kj
    local = (dist >= 0) & (dist < WINDOW)
    bias = dist_bias[:, jnp.clip(dist, 0, WINDOW - 1)].astype(F32)
    bias = jnp.where(local[None], bias, -jnp.inf)
    bias = bias.reshape(ATTN_KV_HEADS, ATTN_GROUP * BLOCK, 2 * BLOCK)
    sink = jnp.repeat(sinks.astype(F32), BLOCK).reshape(ATTN_KV_HEADS, ATTN_GROUP * BLOCK, 1)
    return bias, sink


def _relayout_w_in(w_in, b_in):
    def cols(t):
        q = t[..., 2 * D_MODEL:2 * D_MODEL + ATTN_WIDTH] * (HEAD_DIM ** -0.5)
        k = t[..., 2 * D_MODEL + ATTN_WIDTH:2 * D_MODEL + ATTN_WIDTH + KV_WIDTH]
        v = t[..., 2 * D_MODEL + ATTN_WIDTH + KV_WIDTH:2 * D_MODEL + ATTN_WIDTH + 2 * KV_WIDTH]
        dup = lambda z: jnp.concatenate(
            [z[..., h * HEAD_DIM:(h + 1) * HEAD_DIM] for h in range(ATTN_KV_HEADS) for _ in (0, 1)],
            axis=-1)
        return jnp.concatenate([t[..., :2 * D_MODEL], q, dup(k), dup(v),
                                t[..., 2 * D_MODEL + ATTN_WIDTH + 2 * KV_WIDTH:]], axis=-1)
    return cols(w_in).astype(BF16), cols(b_in)[None, :]


def _relayout_ffn(w_gu, w_down):
    gate = w_gu[:, :D_FF].reshape(D_MODEL, N_FF_CHUNKS, FF_CHUNK)
    up = w_gu[:, D_FF:].reshape(D_MODEL, N_FF_CHUNKS, FF_CHUNK)
    wgu = jnp.concatenate([gate, up], axis=-1).transpose(1, 0, 2).astype(BF16)
    return wgu, w_down.reshape(N_FF_CHUNKS, FF_CHUNK, D_MODEL).astype(BF16)


def kernel(x, ffn_w_gu, ffn_w_down, ln_g, ln_b, w_in, b_in, rel_bias, attn_sinks, shift_mu,
           rw_w0, rw_w2, rw_a0, rw_a2, rw_g2, rw_k_k, rw_k_a, rw_r_k, rw_gn_g, rw_gn_b,
           rw_v0, rw_v1, rw_v2, w_branch_attn, w_branch_rwkv, w_out):
    batch, seq_len, _ = x.shape
    depth = w_in.shape[0]
    assert seq_len % TM == 0 and seq_len % TQ == 0 and seq_len % WKV_TBLK == 0
    row = lambda t: t.reshape(1, -1)
    h = x.reshape(batch * seq_len, D_MODEL)
    v_first = None
    for l in range(depth):
        h = _ffn_ln(h, *_relayout_ffn(ffn_w_gu[l, 0], ffn_w_down[l, 0]),
                    row(ln_g[l, 0]), row(ln_b[l, 0]))
        vmix = None if l == 0 else (v_first, row(rw_v0[l - 1]), rw_v1[l - 1].astype(BF16),
                                    rw_v2[l - 1].astype(BF16))
        w, b = _relayout_w_in(w_in[l], b_in[l])
        gab, q, kd, vd, r, k, v, lw, a, g = _proj_prep(
            h, seq_len, w, b, row(shift_mu[l]), row(rw_w0[l]), rw_w2[l].astype(BF16),
            row(rw_a0[l]), rw_a2[l].astype(BF16), rw_g2[l].astype(BF16), vmix)
        if l == 0:
            v_first = v
        bias, sink = _attention_tables(rel_bias, attn_sinks[l])
        attn = _swa(q, kd, vd, bias, sink, batch, seq_len)
        rw = _wkv7(r, k, v, lw, a, g, row(rw_k_k[l]), row(rw_k_a[l]), row(rw_r_k[l]),
                   row(rw_gn_g[l]), row(rw_gn_b[l]), batch, seq_len)
        h = _mix_ln(attn, rw, gab, h, w_branch_attn[l].astype(BF16),
                    w_branch_rwkv[l].astype(BF16), w_out[l].astype(BF16),
                    row(ln_g[l, 1]), row(ln_b[l, 1]))
        h = _ffn_ln(h, *_relayout_ffn(ffn_w_gu[l, 1], ffn_w_down[l, 1]),
                    row(ln_g[l, 2]), row(ln_b[l, 2]))
    return h.reshape(batch, seq_len, D_MODEL)
```

```python
import functools
import math

import jax
import jax.numpy as jnp
from jax import lax
from jax.experimental import pallas as pl
from jax.experimental.pallas import tpu as pltpu

F32 = jnp.float32
BF16 = jnp.bfloat16

D_MODEL = 1024
DEPTH = 2
HEAD_DIM = 64
ATTN_Q_HEADS = 8
ATTN_KV_HEADS = 2
ATTN_GROUP = ATTN_Q_HEADS // ATTN_KV_HEADS
ATTN_WIDTH = ATTN_Q_HEADS * HEAD_DIM
KV_WIDTH = ATTN_KV_HEADS * HEAD_DIM
WINDOW = 128
BLOCK = 128
N_BUCKETS = 32
MAX_DISTANCE = 128
RW_HEADS = 8
RW_HEAD = 64
RW_WIDTH = RW_HEADS * RW_HEAD
DECAY_RANK = 64
A_RANK = 64
V_RANK = 32
G_RANK = 128
GN_EPS = 64e-5
D_FF = 2816
ALPHA = (2 * DEPTH) ** 0.25
LN_EPS = 1e-5
RW_COLS = 3 * RW_WIDTH + DECAY_RANK + A_RANK + G_RANK

LANES = 128
MXU_DIM = 256
VMEM_LIMIT = 56 * 1024 * 1024

TM = 512
FF_CHUNK = MXU_DIM
N_FF_CHUNKS = D_FF // FF_CHUNK
TQ = 512
CHUNK = 64
WKV_TBLK = 256
PAIR = 2 * RW_HEAD
N_PAIRS = RW_WIDTH // PAIR

PW_GATES = 2 * D_MODEL
PW_Q = ATTN_WIDTH
PW_KD = 2 * KV_WIDTH
PW_VD = 2 * KV_WIDTH
PO_Q = PW_GATES
PO_KD = PO_Q + PW_Q
PO_VD = PO_KD + PW_KD
PO_RW = PO_VD + PW_VD
PROJ_COLS = PO_RW + RW_COLS


def _dot(a, b):
    return jnp.dot(a, b, preferred_element_type=F32)


def _dot_nt(a, b):
    return lax.dot_general(a, b, (((1,), (1,)), ((), ())), preferred_element_type=F32)


def _layer_norm(y, g, b):
    mu = jnp.mean(y, axis=-1, keepdims=True)
    d = y - mu
    var = jnp.mean(d * d, axis=-1, keepdims=True)
    return d * lax.rsqrt(var + LN_EPS) * g + b


def _resident(shape):
    nd = len(shape)
    return pl.BlockSpec(shape, lambda *_: (0,) * nd, pipeline_mode=pl.Buffered(1))


def _params(*sem):
    return pltpu.CompilerParams(dimension_semantics=sem, vmem_limit_bytes=VMEM_LIMIT)


def _ffn_ln_body(x_ref, wgu_ref, wd_ref, g_ref, b_ref, o_ref):
    x = x_ref[...]
    xb = x.astype(BF16)
    acc = jnp.zeros(x.shape, F32)
    for j in range(N_FF_CHUNKS):
        gu = _dot(xb, wgu_ref[j])
        h = jax.nn.silu(gu[:, :FF_CHUNK]) * gu[:, FF_CHUNK:]
        acc = acc + _dot(h.astype(BF16), wd_ref[j])
    o_ref[...] = _layer_norm(ALPHA * x + 0.5 * acc, g_ref[...], b_ref[...])


def _ffn_ln(x, wgu, wd, g, b):
    m = x.shape[0]
    return pl.pallas_call(
        _ffn_ln_body,
        grid=(m // TM,),
        in_specs=[pl.BlockSpec((TM, D_MODEL), lambda i: (i, 0)),
                  _resident(wgu.shape), _resident(wd.shape),
                  _resident(g.shape), _resident(b.shape)],
        out_specs=pl.BlockSpec((TM, D_MODEL), lambda i: (i, 0)),
        out_shape=jax.ShapeDtypeStruct((m, D_MODEL), F32),
        compiler_params=_params("parallel"),
        name="ffn_ln",
    )(x, wgu, wd, g, b)


def _softplus(z):
    return jnp.maximum(z, 0.0) + jnp.log1p(jnp.exp(-jnp.abs(z)))


def _proj_body(tiles_per_seq, has_vmix, *refs):
    (x_ref, w_ref, b_ref, mu_ref, w0_ref, w2_ref, a0_ref, a2_ref, g2_ref) = refs[:9]
    refs = refs[9:]
    if has_vmix:
        vf_ref, v0_ref, v1_ref, v2_ref = refs[:4]
        refs = refs[4:]
    (gab_ref, q_ref, kd_ref, vd_ref, r_ref, k_ref, v_ref, lw_ref, a_ref, g_ref, carry_ref) = refs

    i = pl.program_id(0)
    tm = x_ref.shape[0]
    proj = _dot(x_ref[...].astype(BF16), w_ref[...]) + b_ref[...]
    gab_ref[...] = jax.nn.sigmoid(proj[:, :PW_GATES]).astype(BF16)
    q_ref[...] = proj[:, PO_Q:PO_KD].astype(BF16)
    kd_ref[...] = proj[:, PO_KD:PO_VD].astype(BF16)
    vd_ref[...] = proj[:, PO_VD:PO_RW].astype(BF16)

    u = proj[:, PO_RW:]

    @pl.when(i % tiles_per_seq == 0)
    def _():
        carry_ref[...] = jnp.zeros(carry_ref.shape, F32)

    before_tile = carry_ref[7:8, :]
    carry_ref[...] = u[tm - 8:, :]
    row = lax.broadcasted_iota(jnp.int32, (tm, 1), 0)
    u_prev = jnp.where(row == 0, before_tile, pltpu.roll(u, 1, 0))
    u = u + (u_prev - u) * mu_ref[...]

    c = RW_WIDTH
    r = u[:, :c]
    k = u[:, c:2 * c]
    v = u[:, 2 * c:3 * c]
    o = 3 * c
    xw = u[:, o:o + DECAY_RANK]
    xa = u[:, o + DECAY_RANK:o + DECAY_RANK + A_RANK]
    xg = u[:, o + DECAY_RANK + A_RANK:]

    w = -_softplus(-(w0_ref[...] + _dot(jnp.tanh(xw).astype(BF16), w2_ref[...]))) - 0.5
    lw_ref[...] = -jnp.exp(w)
    a_ref[...] = jax.nn.sigmoid(a0_ref[...] + _dot(xa.astype(BF16), a2_ref[...]))
    g_ref[...] = _dot(jax.nn.sigmoid(xg).astype(BF16), g2_ref[...]).astype(BF16)
    if has_vmix:
        low = _dot(v.astype(BF16), v1_ref[...])
        mix = jax.nn.sigmoid(v0_ref[...] + _dot(low.astype(BF16), v2_ref[...]))
        v = v + (vf_ref[...] - v) * mix
    r_ref[...] = r
    k_ref[...] = k
    v_ref[...] = v


def _proj_prep(x, seq_len, w, b, mu, w0, w2, a0, a2, g2, vmix):
    m = x.shape[0]
    tile = lambda n: pl.BlockSpec((TM, n), lambda i: (i, 0))
    has_vmix = vmix is not None
    args = [x, w, b, mu, w0, w2, a0, a2, g2]
    in_specs = [tile(D_MODEL)] + [_resident(t.shape) for t in args[1:]]
    if has_vmix:
        vf, v0, v1, v2 = vmix
        args += [vf, v0, v1, v2]
        in_specs += [tile(RW_WIDTH), _resident(v0.shape), _resident(v1.shape), _resident(v2.shape)]
    widths = [(PW_GATES, BF16), (PW_Q, BF16), (PW_KD, BF16), (PW_VD, BF16)] \
        + [(RW_WIDTH, F32)] * 5 + [(RW_WIDTH, BF16)]
    return pl.pallas_call(
        functools.partial(_proj_body, seq_len // TM, has_vmix),
        grid=(m // TM,),
        in_specs=in_specs,
        out_specs=[tile(n) for n, _ in widths],
        out_shape=[jax.ShapeDtypeStruct((m, n), dt) for n, dt in widths],
        scratch_shapes=[pltpu.VMEM((8, RW_COLS), F32)],
        compiler_params=_params("arbitrary"),
        name="proj_prep",
    )(*args)


def _swa_body(q_ref, kc_ref, kp_ref, vc_ref, vp_ref, bias_ref, sink_ref, o_ref):
    i = pl.program_id(1)
    lane = lax.broadcasted_iota(jnp.int32, (1, LANES), 1)
    lo = lane < HEAD_DIM
    key = lax.broadcasted_iota(jnp.int32, (1, 2 * BLOCK), 1)
    no_prev = jnp.logical_and(i == 0, key < BLOCK)
    zero = jnp.zeros((), BF16)
    for j in range(TQ // BLOCK):
        rows = slice(j * BLOCK, (j + 1) * BLOCK)
        if j == 0:
            k_prev, v_prev = kp_ref[...], vp_ref[...]
        else:
            before = slice((j - 1) * BLOCK, j * BLOCK)
            k_prev, v_prev = kc_ref[before, :], vc_ref[before, :]
        k_band = jnp.concatenate([k_prev, kc_ref[rows, :]], axis=0)
        v_band = jnp.concatenate([v_prev, vc_ref[rows, :]], axis=0)
        for h in range(ATTN_KV_HEADS):
            cols = slice(h * LANES, (h + 1) * LANES)
            kb, vb = k_band[:, cols], v_band[:, cols]
            parts = []
            for p in (2 * h, 2 * h + 1):
                qp = q_ref[rows, p * LANES:(p + 1) * LANES]
                parts += [jnp.where(lo, qp, zero), jnp.where(lo, zero, qp)]
            s = _dot_nt(jnp.concatenate(parts, axis=0), kb) + bias_ref[h]
            if j == 0:
                s = jnp.where(no_prev, -jnp.inf, s)
            sink = sink_ref[h]
            mx = jnp.maximum(jnp.max(s, axis=-1, keepdims=True), sink)
            e = jnp.exp(s - mx)
            den = jnp.sum(e, axis=-1, keepdims=True) + jnp.exp(sink - mx)
            e = e.astype(BF16)
            v_lo, v_hi = jnp.where(lo, vb, zero), jnp.where(lo, zero, vb)
            for n, p in enumerate((2 * h, 2 * h + 1)):
                top = slice(2 * n * BLOCK, (2 * n + 1) * BLOCK)
                bot = slice((2 * n + 1) * BLOCK, (2 * n + 2) * BLOCK)
                out = _dot(e[top], v_lo) / den[top] + _dot(e[bot], v_hi) / den[bot]
                o_ref[rows, p * LANES:(p + 1) * LANES] = out.astype(BF16)


def _swa(q, kd, vd, bias, sink, batch, seq_len):
    q3 = q.reshape(batch, seq_len, PW_Q)
    kd3 = kd.reshape(batch, seq_len, PW_KD)
    vd3 = vd.reshape(batch, seq_len, PW_VD)
    cur = lambda n: pl.BlockSpec((None, TQ, n), lambda b, i: (b, i, 0))
    prev = lambda n: pl.BlockSpec(
        (None, BLOCK, n), lambda b, i: (b, jnp.maximum(i * (TQ // BLOCK) - 1, 0), 0))
    out = pl.pallas_call(
        _swa_body,
        grid=(batch, seq_len // TQ),
        in_specs=[cur(PW_Q), cur(PW_KD), prev(PW_KD), cur(PW_VD), prev(PW_VD),
                  _resident(bias.shape), _resident(sink.shape)],
        out_specs=cur(ATTN_WIDTH),
        out_shape=jax.ShapeDtypeStruct((batch, seq_len, ATTN_WIDTH), BF16),
        compiler_params=_params("parallel", "parallel"),
        name="swa",
    )(q3, kd3, kd3, vd3, vd3, bias, sink)
    return out.reshape(batch * seq_len, ATTN_WIDTH)


def _split3(x):
    h1 = x.astype(BF16)
    r1 = x - h1.astype(F32)
    h2 = r1.astype(BF16)
    h3 = (r1 - h2.astype(F32)).astype(BF16)
    return h1, h2, h3


def _wkv_body(r_ref, k_ref, v_ref, lw_ref, a_ref, g_ref,
              kk_ref, ka_ref, rk_ref, gng_ref, gnb_ref, o_ref, s_ref):
    t = pl.program_id(1)
    tblk, width = r_ref.shape
    n_pairs = width // PAIR

    @pl.when(t == 0)
    def _():
        s_ref[...] = jnp.zeros(s_ref.shape, F32)

    lane = lax.broadcasted_iota(jnp.int32, (1, PAIR), 1)
    lo = lane < RW_HEAD
    ri = lax.broadcasted_iota(jnp.int32, (PAIR, PAIR), 0)
    ci = lax.broadcasted_iota(jnp.int32, (PAIR, PAIR), 1)
    same_head = (ri >= RW_HEAD) == (ci >= RW_HEAD)
    head_ones = jnp.where(same_head, 1.0, 0.0).astype(BF16)
    same_blk = (ri >= CHUNK) == (ci >= CHUNK)
    tok_r, tok_c = ri & (CHUNK - 1), ci & (CHUNK - 1)
    strict = jnp.logical_and(same_blk, tok_c < tok_r)
    incl = jnp.logical_and(same_blk, tok_c <= tok_r)
    eye = ri == ci
    rt_ = lax.broadcasted_iota(jnp.int32, (tblk, tblk), 0)
    ct_ = lax.broadcasted_iota(jnp.int32, (tblk, tblk), 1)
    same_chunk = (rt_ & -CHUNK) == (ct_ & -CHUNK)
    chunk_tril = jnp.where(jnp.logical_and(same_chunk, ct_ <= rt_), 1.0, 0.0).astype(BF16)

    def seg_sum(x):
        xb = x.astype(BF16)
        return jnp.concatenate(
            [_dot(xb[:, p * PAIR:(p + 1) * PAIR], head_ones) for p in range(n_pairs)], axis=1)

    def stack(x):
        return jnp.concatenate([jnp.where(lo, x, 0.0), jnp.where(lo, 0.0, x)], axis=0)

    r, k, v, lw, a = r_ref[...], k_ref[...], v_ref[...], lw_ref[...], a_ref[...]
    kk = k * kk_ref[...]
    kk = kk / jnp.maximum(jnp.sqrt(seg_sum(kk * kk)), 1e-12)
    k = k * (1.0 + (a - 1.0) * ka_ref[...])
    b = kk * a
    bonus = seg_sum(r * k * rk_ref[...]) * v

    h1, h2, h3 = _split3(lw)
    cum = _dot(chunk_tril, h1) + _dot(chunk_tril, h2) + _dot(chunk_tril, h3)

    n_chunks = tblk // CHUNK
    units = [(p, c) for c in range(n_chunks) for p in range(n_pairs)]
    idx = range(len(units))
    n_levels = int(math.log2(CHUNK))
    at, rt, bt, kt, bh, kh, vs, dec = [], [], [], [], [], [], [], []
    for p, c in units:
        rows, cols = slice(c * CHUNK, (c + 1) * CHUNK), slice(p * PAIR, (p + 1) * PAIR)
        cu, lwu = cum[rows, cols], lw[rows, cols]
        ru, ku, bu = r[rows, cols], k[rows, cols], b[rows, cols]
        end = cu[CHUNK - 1:CHUNK, :]
        e_pos, e_neg = jnp.exp(cu), jnp.exp(-cu)
        e_prev, e_end = jnp.exp(cu - lwu), jnp.exp(end - cu)
        at.append(stack(-kk[rows, cols] * e_prev))
        rt.append(stack(ru * e_pos))
        bt.append(stack(bu * e_neg).astype(BF16))
        kt.append(stack(ku * e_neg).astype(BF16))
        bh.append(jnp.transpose(stack(bu * e_end)).astype(BF16))
        kh.append(jnp.transpose(stack(ku * e_end)).astype(BF16))
        vs.append(stack(v[rows, cols]).astype(BF16))
        dec.append(jnp.where(eye, jnp.exp(end), 0.0))

    lhs = [jnp.concatenate([at[u], rt[u]], axis=0).astype(BF16) for u in idx]
    gb = [_dot_nt(lhs[u], bt[u]) for u in idx]
    gk = [_dot_nt(lhs[u], kt[u]) for u in idx]
    a_kv = [jnp.concatenate([jnp.where(strict, gk[u][:PAIR], 0.0),
                             jnp.where(incl, gk[u][PAIR:], 0.0)], axis=0).astype(BF16) for u in idx]
    akv = [_dot(a_kv[u], vs[u]) for u in idx]
    khv = [_dot(kh[u], vs[u]) for u in idx]
    a_rb = [jnp.where(incl, gb[u][PAIR:], 0.0).astype(BF16) for u in idx]

    pw = [jnp.where(strict, gb[u][:PAIR], 0.0).astype(BF16) for u in idx]
    x = [jnp.concatenate([at[u], akv[u][:PAIR]], axis=1) for u in idx]
    for level in range(n_levels):
        x = [x[u] + _dot(pw[u], x[u].astype(BF16)) for u in idx]
        if level + 1 < n_levels:
            pw = [_dot(pw[u], pw[u]).astype(BF16) for u in idx]
    xb = [x[u].astype(BF16) for u in idx]

    qy = [_dot(a_rb[u], xb[u]) + jnp.concatenate([rt[u], akv[u][PAIR:]], axis=1) for u in idx]
    mc = [_dot(bh[u], xb[u]) + jnp.concatenate([dec[u], khv[u]], axis=1) for u in idx]
    q_hat = [qy[u][:, :PAIR].astype(BF16) for u in idx]
    m_hat = [mc[u][:, :PAIR].astype(BF16) for u in idx]

    state = [s_ref[p] for p in range(n_pairs)]
    ys = [[None] * n_chunks for _ in range(n_pairs)]
    for u, (p, c) in enumerate(units):
        sb = state[p].astype(BF16)
        y = _dot(q_hat[u], sb) + qy[u][:, PAIR:]
        ys[p][c] = y[:CHUNK] + y[CHUNK:]
        state[p] = _dot(m_hat[u], sb) + mc[u][:, PAIR:]
    for p in range(n_pairs):
        s_ref[p] = state[p]

    y = jnp.concatenate([jnp.concatenate(ys[p], axis=0) for p in range(n_pairs)], axis=1)
    yh, yl, _ = _split3(y)
    inv_n = 1.0 / RW_HEAD
    mu = (seg_sum(yh) + seg_sum(yl)) * inv_n
    d = y - mu
    var = seg_sum(d * d) * inv_n
    y = d * lax.rsqrt(var + GN_EPS) * gng_ref[...] + gnb_ref[...]
    o_ref[...] = ((y + bonus) * g_ref[...].astype(F32)).astype(BF16)


def _wkv7(r, k, v, lw, a, g, k_k, k_a, r_k, gn_g, gn_b, batch, seq_len):
    tok = pl.BlockSpec((None, WKV_TBLK, RW_WIDTH), lambda b, t: (b, t, 0))
    shape3 = (batch, seq_len, RW_WIDTH)
    params = (k_k, k_a, r_k, gn_g, gn_b)
    out = pl.pallas_call(
        _wkv_body,
        grid=(batch, seq_len // WKV_TBLK),
        in_specs=[tok] * 6 + [_resident(t.shape) for t in params],
        out_specs=tok,
        out_shape=jax.ShapeDtypeStruct(shape3, BF16),
        scratch_shapes=[pltpu.VMEM((N_PAIRS, PAIR, PAIR), F32)],
        compiler_params=_params("parallel", "arbitrary"),
        name="wkv7",
    )(*(t.reshape(shape3) for t in (r, k, v, lw, a, g)), *params)
    return out.reshape(batch * seq_len, RW_WIDTH)


def _mix_ln_body(attn_ref, rw_ref, gab_ref, x_ref, wba_ref, wbb_ref, wo_ref, g_ref, b_ref, o_ref):
    gab = gab_ref[...]
    merged = (gab[:, :D_MODEL].astype(F32) * _dot(attn_ref[...], wba_ref[...])
              + gab[:, D_MODEL:].astype(F32) * _dot(rw_ref[...], wbb_ref[...]))
    mix = _dot(merged.astype(BF16), wo_ref[...])
    o_ref[...] = _layer_norm(ALPHA * x_ref[...] + mix, g_ref[...], b_ref[...])


def _mix_ln(attn, rw, gab, x, wba, wbb, wo, g, b):
    m = x.shape[0]
    tile = lambda n: pl.BlockSpec((TM, n), lambda i: (i, 0))
    return pl.pallas_call(
        _mix_ln_body,
        grid=(m // TM,),
        in_specs=[tile(ATTN_WIDTH), tile(RW_WIDTH), tile(PW_GATES), tile(D_MODEL)]
        + [_resident(t.shape) for t in (wba, wbb, wo, g, b)],
        out_specs=tile(D_MODEL),
        out_shape=jax.ShapeDtypeStruct((m, D_MODEL), F32),
        compiler_params=_params("parallel"),
        name="mix_ln",
    )(attn, rw, gab, x, wba, wbb, wo, g, b)


def _t5_bucket(n):
    max_exact = N_BUCKETS // 2
    nf = jnp.maximum(n, 1).astype(F32)
    large = max_exact + (jnp.log(nf / max_exact) / math.log(MAX_DISTANCE / max_exact)
                         * (N_BUCKETS - max_exact)).astype(jnp.int32)
    large = jnp.minimum(large, N_BUCKETS - 1)
    return jnp.where(n < max_exact, n, large)


def _attention_tables(rel_bias, sinks):
    dist_bias = rel_bias[_t5_bucket(jnp.arange(WINDOW, dtype=jnp.int32))].T
    qi = jnp.arange(BLOCK)[:, None]
    kj = jnp.arange(2 * BLOCK)[None, :]
    dist = qi + BLOCK - kj
    local = (dist >= 0) & (dist < WINDOW)
    bias = dist_bias[:, jnp.clip(dist, 0, WINDOW - 1)].astype(F32)
    bias = jnp.where(local[None], bias, -jnp.inf)
    bias = bias.reshape(ATTN_KV_HEADS, ATTN_GROUP * BLOCK, 2 * BLOCK)
    sink = jnp.repeat(sinks.astype(F32), BLOCK).reshape(ATTN_KV_HEADS, ATTN_GROUP * BLOCK, 1)
    return bias, sink


def _relayout_w_in(w_in, b_in):
    def cols(t):
        q = t[..., 2 * D_MODEL:2 * D_MODEL + ATTN_WIDTH] * (HEAD_DIM ** -0.5)
        k = t[..., 2 * D_MODEL + ATTN_WIDTH:2 * D_MODEL + ATTN_WIDTH + KV_WIDTH]
        v = t[..., 2 * D_MODEL + ATTN_WIDTH + KV_WIDTH:2 * D_MODEL + ATTN_WIDTH + 2 * KV_WIDTH]
        dup = lambda z: jnp.concatenate(
            [z[..., h * HEAD_DIM:(h + 1) * HEAD_DIM] for h in range(ATTN_KV_HEADS) for _ in (0, 1)],
            axis=-1)
        return jnp.concatenate([t[..., :2 * D_MODEL], q, dup(k), dup(v),
                                t[..., 2 * D_MODEL + ATTN_WIDTH + 2 * KV_WIDTH:]], axis=-1)
    return cols(w_in).astype(BF16), cols(b_in)[None, :]


def _relayout_ffn(w_gu, w_down):
    gate = w_gu[:, :D_FF].reshape(D_MODEL, N_FF_CHUNKS, FF_CHUNK)
    up = w_gu[:, D_FF:].reshape(D_MODEL, N_FF_CHUNKS, FF_CHUNK)
    wgu = jnp.concatenate([gate, up], axis=-1).transpose(1, 0, 2).astype(BF16)
    return wgu, w_down.reshape(N_FF_CHUNKS, FF_CHUNK, D_MODEL).astype(BF16)


def kernel(x, ffn_w_gu, ffn_w_down, ln_g, ln_b, w_in, b_in, rel_bias, attn_sinks, shift_mu,
           rw_w0, rw_w2, rw_a0, rw_a2, rw_g2, rw_k_k, rw_k_a, rw_r_k, rw_gn_g, rw_gn_b,
           rw_v0, rw_v1, rw_v2, w_branch_attn, w_branch_rwkv, w_out):
    batch, seq_len, _ = x.shape
    depth = w_in.shape[0]
    assert seq_len % TM == 0 and seq_len % TQ == 0 and seq_len % WKV_TBLK == 0
    row = lambda t: t.reshape(1, -1)
    h = x.reshape(batch * seq_len, D_MODEL)
    v_first = None
    for l in range(depth):
        h = _ffn_ln(h, *_relayout_ffn(ffn_w_gu[l, 0], ffn_w_down[l, 0]),
                    row(ln_g[l, 0]), row(ln_b[l, 0]))
        vmix = None if l == 0 else (v_first, row(rw_v0[l - 1]), rw_v1[l - 1].astype(BF16),
                                    rw_v2[l - 1].astype(BF16))
        w, b = _relayout_w_in(w_in[l], b_in[l])
        gab, q, kd, vd, r, k, v, lw, a, g = _proj_prep(
            h, seq_len, w, b, row(shift_mu[l]), row(rw_w0[l]), rw_w2[l].astype(BF16),
            row(rw_a0[l]), rw_a2[l].astype(BF16), rw_g2[l].astype(BF16), vmix)
        if l == 0:
            v_first = v
        bias, sink = _attention_tables(rel_bias, attn_sinks[l])
        attn = _swa(q, kd, vd, bias, sink, batch, seq_len)
        rw = _wkv7(r, k, v, lw, a, g, row(rw_k_k[l]), row(rw_k_a[l]), row(rw_r_k[l]),
                   row(rw_gn_g[l]), row(rw_gn_b[l]), batch, seq_len)
        h = _mix_ln(attn, rw, gab, h, w_branch_attn[l].astype(BF16),
                    w_branch_rwkv[l].astype(BF16), w_out[l].astype(BF16),
                    row(ln_g[l, 1]), row(ln_b[l, 1]))
        h = _ffn_ln(h, *_relayout_ffn(ffn_w_gu[l, 1], ffn_w_down[l, 1]),
                    row(ln_g[l, 2]), row(ln_b[l, 2]))
    return h.reshape(batch, seq_len, D_MODEL)
```

```python
import functools
import math

import jax
import jax.numpy as jnp
from jax import lax
from jax.experimental import pallas as pl
from jax.experimental.pallas import tpu as pltpu

F32 = jnp.float32
BF16 = jnp.bfloat16

D_MODEL = 1024
DEPTH = 2
HEAD_DIM = 64
ATTN_Q_HEADS = 8
ATTN_KV_HEADS = 2
ATTN_GROUP = ATTN_Q_HEADS // ATTN_KV_HEADS
ATTN_WIDTH = ATTN_Q_HEADS * HEAD_DIM
KV_WIDTH = ATTN_KV_HEADS * HEAD_DIM
WINDOW = 128
BLOCK = 128
N_BUCKETS = 32
MAX_DISTANCE = 128
RW_HEADS = 8
RW_HEAD = 64
RW_WIDTH = RW_HEADS * RW_HEAD
DECAY_RANK = 64
A_RANK = 64
V_RANK = 32
G_RANK = 128
GN_EPS = 64e-5
D_FF = 2816
ALPHA = (2 * DEPTH) ** 0.25
LN_EPS = 1e-5
RW_COLS = 3 * RW_WIDTH + DECAY_RANK + A_RANK + G_RANK

LANES = 128
MXU_DIM = 256
VMEM_LIMIT = 56 * 1024 * 1024

TM = 512
FF_CHUNK = MXU_DIM
N_FF_CHUNKS = D_FF // FF_CHUNK
TQ = 512
CHUNK = 64
WKV_TBLK = 256
PAIR = 2 * RW_HEAD
N_PAIRS = RW_WIDTH // PAIR

PW_GATES = 2 * D_MODEL
PW_Q = ATTN_WIDTH
PW_KD = 2 * KV_WIDTH
PW_VD = 2 * KV_WIDTH
PO_Q = PW_GATES
PO_KD = PO_Q + PW_Q
PO_VD = PO_KD + PW_KD
PO_RW = PO_VD + PW_VD
PROJ_COLS = PO_RW + RW_COLS


def _dot(a, b):
    return jnp.dot(a, b, preferred_element_type=F32)


def _dot_nt(a, b):
    return lax.dot_general(a, b, (((1,), (1,)), ((), ())), preferred_element_type=F32)


def _layer_norm(y, g, b):
    mu = jnp.mean(y, axis=-1, keepdims=True)
    d = y - mu
    var = jnp.mean(d * d, axis=-1, keepdims=True)
    return d * lax.rsqrt(var + LN_EPS) * g + b


def _resident(shape):
    nd = len(shape)
    return pl.BlockSpec(shape, lambda *_: (0,) * nd, pipeline_mode=pl.Buffered(1))


def _params(*sem):
    return pltpu.CompilerParams(dimension_semantics=sem, vmem_limit_bytes=VMEM_LIMIT)


def _ffn_ln_body(x_ref, wgu_ref, wd_ref, g_ref, b_ref, o_ref):
    x = x_ref[...]
    xb = x.astype(BF16)
    acc = jnp.zeros(x.shape, F32)
    for j in range(N_FF_CHUNKS):
        cols = slice(j * FF_CHUNK, (j + 1) * FF_CHUNK)
        gate = _dot(xb, wgu_ref[:, cols])
        up = _dot(xb, wgu_ref[:, D_FF + j * FF_CHUNK:D_FF + (j + 1) * FF_CHUNK])
        h = jax.nn.silu(gate) * up
        acc = acc + _dot(h.astype(BF16), wd_ref[cols, :])
    o_ref[...] = _layer_norm(ALPHA * x + 0.5 * acc, g_ref[...], b_ref[...])


def _ffn_ln(x, wgu, wd, g, b):
    m = x.shape[0]
    return pl.pallas_call(
        _ffn_ln_body,
        grid=(m // TM,),
        in_specs=[pl.BlockSpec((TM, D_MODEL), lambda i: (i, 0)),
                  _resident(wgu.shape), _resident(wd.shape),
                  _resident(g.shape), _resident(b.shape)],
        out_specs=pl.BlockSpec((TM, D_MODEL), lambda i: (i, 0)),
        out_shape=jax.ShapeDtypeStruct((m, D_MODEL), F32),
        compiler_params=_params("parallel"),
        name="ffn_ln",
    )(x, wgu, wd, g, b)


def _softplus(z):
    return jnp.maximum(z, 0.0) + jnp.log1p(jnp.exp(-jnp.abs(z)))


def _proj_body(tiles_per_seq, has_vmix, *refs):
    (x_ref, w_ref, b_ref, mu_ref, w0_ref, w2_ref, a0_ref, a2_ref, g2_ref) = refs[:9]
    refs = refs[9:]
    if has_vmix:
        vf_ref, v0_ref, v1_ref, v2_ref = refs[:4]
        refs = refs[4:]
    (gab_ref, q_ref, kd_ref, vd_ref, r_ref, k_ref, v_ref, lw_ref, a_ref, g_ref, carry_ref) = refs

    i = pl.program_id(0)
    tm = x_ref.shape[0]

    @pl.when(i % tiles_per_seq == 0)
    def _():
        carry_ref[...] = jnp.zeros(carry_ref.shape, F32)

    xb = x_ref[...].astype(BF16)
    part = lambda lo, hi: _dot(xb, w_ref[:, lo:hi]) + b_ref[:, lo:hi]
    u = part(PO_RW, PROJ_COLS)
    gates = part(0, PW_GATES)

    before_tile = carry_ref[7:8, :]
    carry_ref[...] = u[tm - 8:, :]
    row = lax.broadcasted_iota(jnp.int32, (tm, 1), 0)
    u_prev = jnp.where(row == 0, before_tile, pltpu.roll(u, 1, 0))
    u = u + (u_prev - u) * mu_ref[...]

    c = RW_WIDTH
    r = u[:, :c]
    k = u[:, c:2 * c]
    v = u[:, 2 * c:3 * c]
    o = 3 * c
    xw = u[:, o:o + DECAY_RANK]
    xa = u[:, o + DECAY_RANK:o + DECAY_RANK + A_RANK]
    xg = u[:, o + DECAY_RANK + A_RANK:]

    w = -_softplus(-(w0_ref[...] + _dot(jnp.tanh(xw).astype(BF16), w2_ref[...]))) - 0.5
    lw_ref[...] = -jnp.exp(w)
    a_ref[...] = jax.nn.sigmoid(a0_ref[...] + _dot(xa.astype(BF16), a2_ref[...]))
    g_ref[...] = _dot(jax.nn.sigmoid(xg).astype(BF16), g2_ref[...]).astype(BF16)
    if has_vmix:
        low = _dot(v.astype(BF16), v1_ref[...])
        mix = jax.nn.sigmoid(v0_ref[...] + _dot(low.astype(BF16), v2_ref[...]))
        v = v + (vf_ref[...] - v) * mix
    r_ref[...] = r
    k_ref[...] = k
    v_ref[...] = v

    qkv = part(PO_Q, PO_RW)
    gab_ref[...] = jax.nn.sigmoid(gates).astype(BF16)
    q_ref[...] = qkv[:, :PW_Q].astype(BF16)
    kd_ref[...] = qkv[:, PW_Q:PW_Q + PW_KD].astype(BF16)
    vd_ref[...] = qkv[:, PW_Q + PW_KD:].astype(BF16)


def _proj_prep(x, seq_len, w, b, mu, w0, w2, a0, a2, g2, vmix):
    m = x.shape[0]
    tile = lambda n: pl.BlockSpec((TM, n), lambda i: (i, 0))
    has_vmix = vmix is not None
    args = [x, w, b, mu, w0, w2, a0, a2, g2]
    in_specs = [tile(D_MODEL)] + [_resident(t.shape) for t in args[1:]]
    if has_vmix:
        vf, v0, v1, v2 = vmix
        args += [vf, v0, v1, v2]
        in_specs += [tile(RW_WIDTH), _resident(v0.shape), _resident(v1.shape), _resident(v2.shape)]
    widths = [(PW_GATES, BF16), (PW_Q, BF16), (PW_KD, BF16), (PW_VD, BF16)] \
        + [(RW_WIDTH, F32)] * 5 + [(RW_WIDTH, BF16)]
    return pl.pallas_call(
        functools.partial(_proj_body, seq_len // TM, has_vmix),
        grid=(m // TM,),
        in_specs=in_specs,
        out_specs=[tile(n) for n, _ in widths],
        out_shape=[jax.ShapeDtypeStruct((m, n), dt) for n, dt in widths],
        scratch_shapes=[pltpu.VMEM((8, RW_COLS), F32)],
        compiler_params=_params("arbitrary"),
        name="proj_prep",
    )(*args)


def _swa_body(q_ref, kc_ref, kp_ref, vc_ref, vp_ref, bias_ref, sink_ref, o_ref):
    i = pl.program_id(1)
    lane = lax.broadcasted_iota(jnp.int32, (1, LANES), 1)
    lo = lane < HEAD_DIM
    key = lax.broadcasted_iota(jnp.int32, (1, 2 * BLOCK), 1)
    no_prev = jnp.logical_and(i == 0, key < BLOCK)
    zero = jnp.zeros((), BF16)
    units = [(j, h) for j in range(TQ // BLOCK) for h in range(ATTN_KV_HEADS)]
    idx = range(len(units))
    scores, v_lo, v_hi = [], [], []
    for j, h in units:
        rows = slice(j * BLOCK, (j + 1) * BLOCK)
        cols = slice(h * LANES, (h + 1) * LANES)
        if j == 0:
            k_prev, v_prev = kp_ref[:, cols], vp_ref[:, cols]
        else:
            before = slice((j - 1) * BLOCK, j * BLOCK)
            k_prev, v_prev = kc_ref[before, cols], vc_ref[before, cols]
        kb = jnp.concatenate([k_prev, kc_ref[rows, cols]], axis=0)
        vb = jnp.concatenate([v_prev, vc_ref[rows, cols]], axis=0)
        parts = []
        for p in (2 * h, 2 * h + 1):
            qp = q_ref[rows, p * LANES:(p + 1) * LANES]
            parts += [jnp.where(lo, qp, zero), jnp.where(lo, zero, qp)]
        s = _dot_nt(jnp.concatenate(parts, axis=0), kb) + bias_ref[h]
        if j == 0:
            s = jnp.where(no_prev, -jnp.inf, s)
        scores.append(s)
        v_lo.append(jnp.where(lo, vb, zero))
        v_hi.append(jnp.where(lo, zero, vb))
    sinks = [sink_ref[h] for _, h in units]
    mx = [jnp.maximum(jnp.max(scores[u], axis=-1, keepdims=True), sinks[u]) for u in idx]
    e = [jnp.exp(scores[u] - mx[u]) for u in idx]
    den = [jnp.sum(e[u], axis=-1, keepdims=True) + jnp.exp(sinks[u] - mx[u]) for u in idx]
    eb = [e[u].astype(BF16) for u in idx]
    for u, (j, h) in enumerate(units):
        rows = slice(j * BLOCK, (j + 1) * BLOCK)
        for n, p in enumerate((2 * h, 2 * h + 1)):
            top = slice(2 * n * BLOCK, (2 * n + 1) * BLOCK)
            bot = slice((2 * n + 1) * BLOCK, (2 * n + 2) * BLOCK)
            out = (_dot(eb[u][top], v_lo[u]) / den[u][top]
                   + _dot(eb[u][bot], v_hi[u]) / den[u][bot])
            o_ref[rows, p * LANES:(p + 1) * LANES] = out.astype(BF16)


def _swa(q, kd, vd, bias, sink, batch, seq_len):
    q3 = q.reshape(batch, seq_len, PW_Q)
    kd3 = kd.reshape(batch, seq_len, PW_KD)
    vd3 = vd.reshape(batch, seq_len, PW_VD)
    cur = lambda n: pl.BlockSpec((None, TQ, n), lambda b, i: (b, i, 0))
    prev = lambda n: pl.BlockSpec(
        (None, BLOCK, n), lambda b, i: (b, jnp.maximum(i * (TQ // BLOCK) - 1, 0), 0))
    out = pl.pallas_call(
        _swa_body,
        grid=(batch, seq_len // TQ),
        in_specs=[cur(PW_Q), cur(PW_KD), prev(PW_KD), cur(PW_VD), prev(PW_VD),
                  _resident(bias.shape), _resident(sink.shape)],
        out_specs=cur(ATTN_WIDTH),
        out_shape=jax.ShapeDtypeStruct((batch, seq_len, ATTN_WIDTH), BF16),
        compiler_params=_params("parallel", "parallel"),
        name="swa",
    )(q3, kd3, kd3, vd3, vd3, bias, sink)
    return out.reshape(batch * seq_len, ATTN_WIDTH)


def _split3(x):
    h1 = x.astype(BF16)
    r1 = x - h1.astype(F32)
    h2 = r1.astype(BF16)
    h3 = (r1 - h2.astype(F32)).astype(BF16)
    return h1, h2, h3


def _wkv_body(r_ref, k_ref, v_ref, lw_ref, a_ref, g_ref,
              kk_ref, ka_ref, rk_ref, gng_ref, gnb_ref, o_ref, s_ref):
    t = pl.program_id(1)
    tblk, width = r_ref.shape
    n_pairs = width // PAIR

    @pl.when(t == 0)
    def _():
        s_ref[...] = jnp.zeros(s_ref.shape, F32)

    lane = lax.broadcasted_iota(jnp.int32, (1, PAIR), 1)
    lo = lane < RW_HEAD
    ri = lax.broadcasted_iota(jnp.int32, (PAIR, PAIR), 0)
    ci = lax.broadcasted_iota(jnp.int32, (PAIR, PAIR), 1)
    same_head = (ri >= RW_HEAD) == (ci >= RW_HEAD)
    head_ones = jnp.where(same_head, 1.0, 0.0).astype(BF16)
    same_blk = (ri >= CHUNK) == (ci >= CHUNK)
    tok_r, tok_c = ri & (CHUNK - 1), ci & (CHUNK - 1)
    strict = jnp.logical_and(same_blk, tok_c < tok_r)
    incl = jnp.logical_and(same_blk, tok_c <= tok_r)
    eye = ri == ci
    rt_ = lax.broadcasted_iota(jnp.int32, (tblk, tblk), 0)
    ct_ = lax.broadcasted_iota(jnp.int32, (tblk, tblk), 1)
    same_chunk = (rt_ & -CHUNK) == (ct_ & -CHUNK)
    chunk_tril = jnp.where(jnp.logical_and(same_chunk, ct_ <= rt_), 1.0, 0.0).astype(BF16)

    def seg_sum(x):
        xb = x.astype(BF16)
        return jnp.concatenate(
            [_dot(xb[:, p * PAIR:(p + 1) * PAIR], head_ones) for p in range(n_pairs)], axis=1)

    def stack(x):
        return jnp.concatenate([jnp.where(lo, x, 0.0), jnp.where(lo, 0.0, x)], axis=0)

    r, k, v, lw, a = r_ref[...], k_ref[...], v_ref[...], lw_ref[...], a_ref[...]
    kk = k * kk_ref[...]
    kk = kk / jnp.maximum(jnp.sqrt(seg_sum(kk * kk)), 1e-12)
    k = k * (1.0 + (a - 1.0) * ka_ref[...])
    b = kk * a
    bonus = seg_sum(r * k * rk_ref[...]) * v

    h1, h2, h3 = _split3(lw)
    cum = _dot(chunk_tril, h1) + _dot(chunk_tril, h2) + _dot(chunk_tril, h3)

    n_chunks = tblk // CHUNK
    units = [(p, c) for c in range(n_chunks) for p in range(n_pairs)]
    idx = range(len(units))
    n_levels = int(math.log2(CHUNK))
    at, rt, bt, kt, bh, kh, vs, dec = [], [], [], [], [], [], [], []
    for p, c in units:
        rows, cols = slice(c * CHUNK, (c + 1) * CHUNK), slice(p * PAIR, (p + 1) * PAIR)
        cu, lwu = cum[rows, cols], lw[rows, cols]
        ru, ku, bu = r[rows, cols], k[rows, cols], b[rows, cols]
        end = cu[CHUNK - 1:CHUNK, :]
        e_pos, e_neg = jnp.exp(cu), jnp.exp(-cu)
        e_prev, e_end = jnp.exp(cu - lwu), jnp.exp(end - cu)
        at.append(stack(-kk[rows, cols] * e_prev))
        rt.append(stack(ru * e_pos))
        bt.append(stack(bu * e_neg).astype(BF16))
        kt.append(stack(ku * e_neg).astype(BF16))
        bh.append(jnp.transpose(stack(bu * e_end)).astype(BF16))
        kh.append(jnp.transpose(stack(ku * e_end)).astype(BF16))
        vs.append(stack(v[rows, cols]).astype(BF16))
        dec.append(jnp.where(eye, jnp.exp(end), 0.0))

    lhs = [jnp.concatenate([at[u], rt[u]], axis=0).astype(BF16) for u in idx]
    gbk = [_dot_nt(lhs[u], jnp.concatenate([bt[u], kt[u]], axis=0)) for u in idx]
    gb = [gbk[u][:, :PAIR] for u in idx]
    gk = [gbk[u][:, PAIR:] for u in idx]
    a_kv = [jnp.concatenate([jnp.where(strict, gk[u][:PAIR], 0.0),
                             jnp.where(incl, gk[u][PAIR:], 0.0)], axis=0).astype(BF16) for u in idx]
    akv = [_dot(a_kv[u], vs[u]) for u in idx]
    khv = [_dot(kh[u], vs[u]) for u in idx]
    a_rb = [jnp.where(incl, gb[u][PAIR:], 0.0).astype(BF16) for u in idx]

    pw = [jnp.where(strict, gb[u][:PAIR], 0.0).astype(BF16) for u in idx]
    x = [jnp.concatenate([at[u], akv[u][:PAIR]], axis=1) for u in idx]
    for level in range(n_levels):
        x = [x[u] + _dot(pw[u], x[u].astype(BF16)) for u in idx]
        if level + 1 < n_levels:
            pw = [_dot(pw[u], pw[u]).astype(BF16) for u in idx]
    xb = [x[u].astype(BF16) for u in idx]

    qy = [_dot(a_rb[u], xb[u]) + jnp.concatenate([rt[u], akv[u][PAIR:]], axis=1) for u in idx]
    mc = [_dot(bh[u], xb[u]) + jnp.concatenate([dec[u], khv[u]], axis=1) for u in idx]
    q_hat = [qy[u][:, :PAIR].astype(BF16) for u in idx]
    m_hat = [mc[u][:, :PAIR].astype(BF16) for u in idx]

    state = [s_ref[p] for p in range(n_pairs)]
    ys = [[None] * n_chunks for _ in range(n_pairs)]
    for u, (p, c) in enumerate(units):
        sb = state[p].astype(BF16)
        y = _dot(q_hat[u], sb) + qy[u][:, PAIR:]
        ys[p][c] = y[:CHUNK] + y[CHUNK:]
        state[p] = _dot(m_hat[u], sb) + mc[u][:, PAIR:]
    for p in range(n_pairs):
        s_ref[p] = state[p]

    y = jnp.concatenate([jnp.concatenate(ys[p], axis=0) for p in range(n_pairs)], axis=1)
    yh, yl, _ = _split3(y)
    inv_n = 1.0 / RW_HEAD
    mu = (seg_sum(yh) + seg_sum(yl)) * inv_n
    d = y - mu
    var = seg_sum(d * d) * inv_n
    y = d * lax.rsqrt(var + GN_EPS) * gng_ref[...] + gnb_ref[...]
    o_ref[...] = ((y + bonus) * g_ref[...].astype(F32)).astype(BF16)


def _wkv7(r, k, v, lw, a, g, k_k, k_a, r_k, gn_g, gn_b, batch, seq_len):
    tok = pl.BlockSpec((None, WKV_TBLK, RW_WIDTH), lambda b, t: (b, t, 0))
    shape3 = (batch, seq_len, RW_WIDTH)
    params = (k_k, k_a, r_k, gn_g, gn_b)
    out = pl.pallas_call(
        _wkv_body,
        grid=(batch, seq_len // WKV_TBLK),
        in_specs=[tok] * 6 + [_resident(t.shape) for t in params],
        out_specs=tok,
        out_shape=jax.ShapeDtypeStruct(shape3, BF16),
        scratch_shapes=[pltpu.VMEM((N_PAIRS, PAIR, PAIR), F32)],
        compiler_params=_params("parallel", "arbitrary"),
        name="wkv7",
    )(*(t.reshape(shape3) for t in (r, k, v, lw, a, g)), *params)
    return out.reshape(batch * seq_len, RW_WIDTH)


def _mix_ln_body(attn_ref, rw_ref, gab_ref, x_ref, wba_ref, wbb_ref, wo_ref, g_ref, b_ref, o_ref):
    gab = gab_ref[...]
    merged = (gab[:, :D_MODEL].astype(F32) * _dot(attn_ref[...], wba_ref[...])
              + gab[:, D_MODEL:].astype(F32) * _dot(rw_ref[...], wbb_ref[...]))
    mix = _dot(merged.astype(BF16), wo_ref[...])
    o_ref[...] = _layer_norm(ALPHA * x_ref[...] + mix, g_ref[...], b_ref[...])


def _mix_ln(attn, rw, gab, x, wba, wbb, wo, g, b):
    m = x.shape[0]
    tile = lambda n: pl.BlockSpec((TM, n), lambda i: (i, 0))
    return pl.pallas_call(
        _mix_ln_body,
        grid=(m // TM,),
        in_specs=[tile(ATTN_WIDTH), tile(RW_WIDTH), tile(PW_GATES), tile(D_MODEL)]
        + [_resident(t.shape) for t in (wba, wbb, wo, g, b)],
        out_specs=tile(D_MODEL),
        out_shape=jax.ShapeDtypeStruct((m, D_MODEL), F32),
        compiler_params=_params("parallel"),
        name="mix_ln",
    )(attn, rw, gab, x, wba, wbb, wo, g, b)


def _t5_bucket(n):
    max_exact = N_BUCKETS // 2
    nf = jnp.maximum(n, 1).astype(F32)
    large = max_exact + (jnp.log(nf / max_exact) / math.log(MAX_DISTANCE / max_exact)
                         * (N_BUCKETS - max_exact)).astype(jnp.int32)
    large = jnp.minimum(large, N_BUCKETS - 1)
    return jnp.where(n < max_exact, n, large)


def _attention_tables(rel_bias, sinks):
    dist_bias = rel_bias[_t5_bucket(jnp.arange(WINDOW, dtype=jnp.int32))].T
    period = 3 * BLOCK
    neg = lambda n: jnp.full((ATTN_Q_HEADS, n), -jnp.inf, F32)
    row0 = jnp.concatenate([neg(1), dist_bias[:, ::-1].astype(F32), neg(period - 1 - WINDOW)], axis=1)
    bias = jnp.tile(row0, (1, BLOCK))[:, :BLOCK * (period - 1)]
    bias = bias.reshape(ATTN_Q_HEADS, BLOCK, period - 1)[:, :, :2 * BLOCK]
    bias = bias.reshape(ATTN_KV_HEADS, ATTN_GROUP * BLOCK, 2 * BLOCK)
    sink = jnp.repeat(sinks.astype(F32), BLOCK).reshape(ATTN_KV_HEADS, ATTN_GROUP * BLOCK, 1)
    return bias, sink


def _relayout_w_in(w_in, b_in):
    def cols(t):
        q = t[..., 2 * D_MODEL:2 * D_MODEL + ATTN_WIDTH] * (HEAD_DIM ** -0.5)
        k = t[..., 2 * D_MODEL + ATTN_WIDTH:2 * D_MODEL + ATTN_WIDTH + KV_WIDTH]
        v = t[..., 2 * D_MODEL + ATTN_WIDTH + KV_WIDTH:2 * D_MODEL + ATTN_WIDTH + 2 * KV_WIDTH]
        dup = lambda z: jnp.concatenate(
            [z[..., h * HEAD_DIM:(h + 1) * HEAD_DIM] for h in range(ATTN_KV_HEADS) for _ in (0, 1)],
            axis=-1)
        return jnp.concatenate([t[..., :2 * D_MODEL], q, dup(k), dup(v),
                                t[..., 2 * D_MODEL + ATTN_WIDTH + 2 * KV_WIDTH:]], axis=-1)
    return cols(w_in).astype(BF16), cols(b_in)[None, :]


def _relayout_ffn(w_gu, w_down):
    return w_gu.astype(BF16), w_down.astype(BF16)


def kernel(x, ffn_w_gu, ffn_w_down, ln_g, ln_b, w_in, b_in, rel_bias, attn_sinks, shift_mu,
           rw_w0, rw_w2, rw_a0, rw_a2, rw_g2, rw_k_k, rw_k_a, rw_r_k, rw_gn_g, rw_gn_b,
           rw_v0, rw_v1, rw_v2, w_branch_attn, w_branch_rwkv, w_out):
    batch, seq_len, _ = x.shape
    depth = w_in.shape[0]
    assert seq_len % TM == 0 and seq_len % TQ == 0 and seq_len % WKV_TBLK == 0
    row = lambda t: t.reshape(1, -1)
    h = x.reshape(batch * seq_len, D_MODEL)
    v_first = None
    for l in range(depth):
        h = _ffn_ln(h, *_relayout_ffn(ffn_w_gu[l, 0], ffn_w_down[l, 0]),
                    row(ln_g[l, 0]), row(ln_b[l, 0]))
        vmix = None if l == 0 else (v_first, row(rw_v0[l - 1]), rw_v1[l - 1].astype(BF16),
                                    rw_v2[l - 1].astype(BF16))
        w, b = _relayout_w_in(w_in[l], b_in[l])
        gab, q, kd, vd, r, k, v, lw, a, g = _proj_prep(
            h, seq_len, w, b, row(shift_mu[l]), row(rw_w0[l]), rw_w2[l].astype(BF16),
            row(rw_a0[l]), rw_a2[l].astype(BF16), rw_g2[l].astype(BF16), vmix)
        if l == 0:
            v_first = v
        bias, sink = _attention_tables(rel_bias, attn_sinks[l])
        attn = _swa(q, kd, vd, bias, sink, batch, seq_len)
        rw = _wkv7(r, k, v, lw, a, g, row(rw_k_k[l]), row(rw_k_a[l]), row(rw_r_k[l]),
                   row(rw_gn_g[l]), row(rw_gn_b[l]), batch, seq_len)
        h = _mix_ln(attn, rw, gab, h, w_branch_attn[l].astype(BF16),
                    w_branch_rwkv[l].astype(BF16), w_out[l].astype(BF16),
                    row(ln_g[l, 1]), row(ln_b[l, 1]))
        h = _ffn_ln(h, *_relayout_ffn(ffn_w_gu[l, 1], ffn_w_down[l, 1]),
                    row(ln_g[l, 2]), row(ln_b[l, 2]))
    return h.reshape(batch, seq_len, D_MODEL)
```

```python
import functools
import math

import jax
import jax.numpy as jnp
from jax import lax
from jax.experimental import pallas as pl
from jax.experimental.pallas import tpu as pltpu

F32 = jnp.float32
BF16 = jnp.bfloat16

D_MODEL = 1024
DEPTH = 2
HEAD_DIM = 64
ATTN_Q_HEADS = 8
ATTN_KV_HEADS = 2
ATTN_GROUP = ATTN_Q_HEADS // ATTN_KV_HEADS
ATTN_WIDTH = ATTN_Q_HEADS * HEAD_DIM
KV_WIDTH = ATTN_KV_HEADS * HEAD_DIM
WINDOW = 128
BLOCK = 128
N_BUCKETS = 32
MAX_DISTANCE = 128
RW_HEADS = 8
RW_HEAD = 64
RW_WIDTH = RW_HEADS * RW_HEAD
DECAY_RANK = 64
A_RANK = 64
V_RANK = 32
G_RANK = 128
GN_EPS = 64e-5
D_FF = 2816
ALPHA = (2 * DEPTH) ** 0.25
LN_EPS = 1e-5
RW_COLS = 3 * RW_WIDTH + DECAY_RANK + A_RANK + G_RANK

LANES = 128
MXU_DIM = 256
VMEM_LIMIT = 56 * 1024 * 1024

TM = 512
FF_CHUNK = MXU_DIM
N_FF_CHUNKS = D_FF // FF_CHUNK
TQ = 512
CHUNK = 64
WKV_TBLK = 256
PAIR = 2 * RW_HEAD
N_PAIRS = RW_WIDTH // PAIR

PW_GATES = 2 * D_MODEL
PW_Q = ATTN_WIDTH
PW_KD = 2 * KV_WIDTH
PW_VD = 2 * KV_WIDTH
PO_Q = PW_GATES
PO_RW = PO_Q + ATTN_WIDTH + 2 * KV_WIDTH
PROJ_COLS = PO_RW + RW_COLS


def _dot(a, b):
    return jnp.dot(a, b, preferred_element_type=F32)


def _dot_nt(a, b):
    return lax.dot_general(a, b, (((1,), (1,)), ((), ())), preferred_element_type=F32)


def _layer_norm(y, g, b):
    mu = jnp.mean(y, axis=-1, keepdims=True)
    d = y - mu
    var = jnp.mean(d * d, axis=-1, keepdims=True)
    return d * lax.rsqrt(var + LN_EPS) * g + b


def _resident(shape, lead=()):
    rest = len(shape) - len(lead)
    block = (None,) * len(lead) + tuple(shape[len(lead):])
    return pl.BlockSpec(block, lambda *_: tuple(lead) + (0,) * rest, pipeline_mode=pl.Buffered(1))


def _params(*sem):
    return pltpu.CompilerParams(dimension_semantics=sem, vmem_limit_bytes=VMEM_LIMIT)


def _ffn_ln_body(x_ref, wgu_ref, wd_ref, g_ref, b_ref, o_ref):
    x = x_ref[...]
    xb = x.astype(BF16)
    acc = jnp.zeros(x.shape, F32)
    for j in range(N_FF_CHUNKS):
        cols = slice(j * FF_CHUNK, (j + 1) * FF_CHUNK)
        gate = _dot(xb, wgu_ref[:, cols])
        up = _dot(xb, wgu_ref[:, D_FF + j * FF_CHUNK:D_FF + (j + 1) * FF_CHUNK])
        h = jax.nn.silu(gate) * up
        acc = acc + _dot(h.astype(BF16), wd_ref[cols, :])
    o_ref[...] = _layer_norm(ALPHA * x + 0.5 * acc, g_ref[...], b_ref[...])


def _ffn_ln(x, wgu, wd, lead, g, b):
    m = x.shape[0]
    return pl.pallas_call(
        _ffn_ln_body,
        grid=(m // TM,),
        in_specs=[pl.BlockSpec((TM, D_MODEL), lambda i: (i, 0)),
                  _resident(wgu.shape, lead), _resident(wd.shape, lead),
                  _resident(g.shape), _resident(b.shape)],
        out_specs=pl.BlockSpec((TM, D_MODEL), lambda i: (i, 0)),
        out_shape=jax.ShapeDtypeStruct((m, D_MODEL), F32),
        compiler_params=_params("parallel"),
        name="ffn_ln",
    )(x, wgu, wd, g, b)


def _softplus(z):
    return jnp.maximum(z, 0.0) + jnp.log1p(jnp.exp(-jnp.abs(z)))


def _proj_body(tiles_per_seq, has_vmix, *refs):
    (x_ref, w_ref, b_ref, mu_ref, w0_ref, w2_ref, a0_ref, a2_ref, g2_ref) = refs[:9]
    refs = refs[9:]
    if has_vmix:
        vf_ref, v0_ref, v1_ref, v2_ref = refs[:4]
        refs = refs[4:]
    (gab_ref, q_ref, kd_ref, vd_ref, r_ref, k_ref, v_ref, lw_ref, a_ref, g_ref, carry_ref) = refs

    i = pl.program_id(0)
    tm = x_ref.shape[0]

    @pl.when(i % tiles_per_seq == 0)
    def _():
        carry_ref[...] = jnp.zeros(carry_ref.shape, F32)

    xb = x_ref[...].astype(BF16)
    part = lambda lo, hi: _dot(xb, w_ref[:, lo:hi]) + b_ref[:, lo:hi]
    u = part(PO_RW, PROJ_COLS)
    gates = part(0, PW_GATES)

    before_tile = carry_ref[7:8, :]
    carry_ref[...] = u[tm - 8:, :]
    row = lax.broadcasted_iota(jnp.int32, (tm, 1), 0)
    u_prev = jnp.where(row == 0, before_tile, pltpu.roll(u, 1, 0))
    u = u + (u_prev - u) * mu_ref[...]

    c = RW_WIDTH
    r = u[:, :c]
    k = u[:, c:2 * c]
    v = u[:, 2 * c:3 * c]
    o = 3 * c
    xw = u[:, o:o + DECAY_RANK]
    xa = u[:, o + DECAY_RANK:o + DECAY_RANK + A_RANK]
    xg = u[:, o + DECAY_RANK + A_RANK:]

    w = -_softplus(-(w0_ref[...] + _dot(jnp.tanh(xw).astype(BF16), w2_ref[...]))) - 0.5
    lw_ref[...] = -jnp.exp(w)
    a_ref[...] = jax.nn.sigmoid(a0_ref[...] + _dot(xa.astype(BF16), a2_ref[...]))
    g_ref[...] = _dot(jax.nn.sigmoid(xg).astype(BF16), g2_ref[...]).astype(BF16)
    if has_vmix:
        low = _dot(v.astype(BF16), v1_ref[...])
        mix = jax.nn.sigmoid(v0_ref[...] + _dot(low.astype(BF16), v2_ref[...]))
        v = v + (vf_ref[...] - v) * mix
    r_ref[...] = r
    k_ref[...] = k
    v_ref[...] = v

    qkv = part(PO_Q, PO_RW)
    gab_ref[...] = jax.nn.sigmoid(gates).astype(BF16)
    q_ref[...] = (qkv[:, :ATTN_WIDTH] * (HEAD_DIM ** -0.5)).astype(BF16)
    lo = lax.broadcasted_iota(jnp.int32, (1, LANES), 1) < HEAD_DIM
    for ref, at in ((kd_ref, ATTN_WIDTH), (vd_ref, ATTN_WIDTH + KV_WIDTH)):
        pair = qkv[:, at:at + KV_WIDTH]
        swapped = pltpu.roll(pair, HEAD_DIM, 1)
        ref[...] = jnp.concatenate([jnp.where(lo, pair, swapped),
                                    jnp.where(lo, swapped, pair)], axis=1).astype(BF16)


def _proj_prep(x, seq_len, layer, w, b, mu, w0, w2, a0, a2, g2, vmix):
    m = x.shape[0]
    tile = lambda n: pl.BlockSpec((TM, n), lambda i: (i, 0))
    has_vmix = vmix is not None
    args = [x, w, b, mu, w0, w2, a0, a2, g2]
    in_specs = [tile(D_MODEL), _resident(w.shape, (layer,))] + [_resident(t.shape) for t in args[2:]]
    if has_vmix:
        vf, v0, v1, v2 = vmix
        args += [vf, v0, v1, v2]
        in_specs += [tile(RW_WIDTH), _resident(v0.shape), _resident(v1.shape), _resident(v2.shape)]
    widths = [(PW_GATES, BF16), (PW_Q, BF16), (PW_KD, BF16), (PW_VD, BF16)] \
        + [(RW_WIDTH, F32)] * 5 + [(RW_WIDTH, BF16)]
    return pl.pallas_call(
        functools.partial(_proj_body, seq_len // TM, has_vmix),
        grid=(m // TM,),
        in_specs=in_specs,
        out_specs=[tile(n) for n, _ in widths],
        out_shape=[jax.ShapeDtypeStruct((m, n), dt) for n, dt in widths],
        scratch_shapes=[pltpu.VMEM((8, RW_COLS), F32)],
        compiler_params=_params("arbitrary"),
        name="proj_prep",
    )(*args)


def _swa_body(q_ref, kc_ref, kp_ref, vc_ref, vp_ref, bias_ref, sink_ref, o_ref):
    i = pl.program_id(1)
    lane = lax.broadcasted_iota(jnp.int32, (1, LANES), 1)
    lo = lane < HEAD_DIM
    key = lax.broadcasted_iota(jnp.int32, (1, 2 * BLOCK), 1)
    no_prev = jnp.logical_and(i == 0, key < BLOCK)
    zero = jnp.zeros((), BF16)
    units = [(j, h) for j in range(TQ // BLOCK) for h in range(ATTN_KV_HEADS)]
    idx = range(len(units))
    scores, v_lo, v_hi = [], [], []
    for j, h in units:
        rows = slice(j * BLOCK, (j + 1) * BLOCK)
        cols = slice(h * LANES, (h + 1) * LANES)
        if j == 0:
            k_prev, v_prev = kp_ref[:, cols], vp_ref[:, cols]
        else:
            before = slice((j - 1) * BLOCK, j * BLOCK)
            k_prev, v_prev = kc_ref[before, cols], vc_ref[before, cols]
        kb = jnp.concatenate([k_prev, kc_ref[rows, cols]], axis=0)
        vb = jnp.concatenate([v_prev, vc_ref[rows, cols]], axis=0)
        parts = []
        for p in (2 * h, 2 * h + 1):
            qp = q_ref[rows, p * LANES:(p + 1) * LANES]
            parts += [jnp.where(lo, qp, zero), jnp.where(lo, zero, qp)]
        s = _dot_nt(jnp.concatenate(parts, axis=0), kb) + bias_ref[h]
        if j == 0:
            s = jnp.where(no_prev, -jnp.inf, s)
        scores.append(s)
        v_lo.append(jnp.where(lo, vb, zero))
        v_hi.append(jnp.where(lo, zero, vb))
    sinks = [sink_ref[h] for _, h in units]
    mx = [jnp.maximum(jnp.max(scores[u], axis=-1, keepdims=True), sinks[u]) for u in idx]
    e = [jnp.exp(scores[u] - mx[u]) for u in idx]
    den = [jnp.sum(e[u], axis=-1, keepdims=True) + jnp.exp(sinks[u] - mx[u]) for u in idx]
    eb = [e[u].astype(BF16) for u in idx]
    for u, (j, h) in enumerate(units):
        rows = slice(j * BLOCK, (j + 1) * BLOCK)
        for n, p in enumerate((2 * h, 2 * h + 1)):
            top = slice(2 * n * BLOCK, (2 * n + 1) * BLOCK)
            bot = slice((2 * n + 1) * BLOCK, (2 * n + 2) * BLOCK)
            out = (_dot(eb[u][top], v_lo[u]) / den[u][top]
                   + _dot(eb[u][bot], v_hi[u]) / den[u][bot])
            o_ref[rows, p * LANES:(p + 1) * LANES] = out.astype(BF16)


def _swa(q, kd, vd, bias, sink, batch, seq_len):
    q3 = q.reshape(batch, seq_len, PW_Q)
    kd3 = kd.reshape(batch, seq_len, PW_KD)
    vd3 = vd.reshape(batch, seq_len, PW_VD)
    cur = lambda n: pl.BlockSpec((None, TQ, n), lambda b, i: (b, i, 0))
    prev = lambda n: pl.BlockSpec(
        (None, BLOCK, n), lambda b, i: (b, jnp.maximum(i * (TQ // BLOCK) - 1, 0), 0))
    out = pl.pallas_call(
        _swa_body,
        grid=(batch, seq_len // TQ),
        in_specs=[cur(PW_Q), cur(PW_KD), prev(PW_KD), cur(PW_VD), prev(PW_VD),
                  _resident(bias.shape), _resident(sink.shape)],
        out_specs=cur(ATTN_WIDTH),
        out_shape=jax.ShapeDtypeStruct((batch, seq_len, ATTN_WIDTH), BF16),
        compiler_params=_params("parallel", "parallel"),
        name="swa",
    )(q3, kd3, kd3, vd3, vd3, bias, sink)
    return out.reshape(batch * seq_len, ATTN_WIDTH)


def _split2(x):
    hi = x.astype(BF16)
    return hi, (x - hi.astype(F32)).astype(BF16)


def _wkv_body(r_ref, k_ref, v_ref, lw_ref, a_ref, g_ref,
              kk_ref, ka_ref, rk_ref, gng_ref, gnb_ref, o_ref, s_ref):
    t = pl.program_id(1)
    tblk, width = r_ref.shape
    n_pairs = width // PAIR

    @pl.when(t == 0)
    def _():
        s_ref[...] = jnp.zeros(s_ref.shape, F32)

    lane = lax.broadcasted_iota(jnp.int32, (1, PAIR), 1)
    lo = lane < RW_HEAD
    ri = lax.broadcasted_iota(jnp.int32, (PAIR, PAIR), 0)
    ci = lax.broadcasted_iota(jnp.int32, (PAIR, PAIR), 1)
    same_head = (ri >= RW_HEAD) == (ci >= RW_HEAD)
    head_ones = jnp.where(same_head, 1.0, 0.0).astype(BF16)
    same_blk = (ri >= CHUNK) == (ci >= CHUNK)
    tok_r, tok_c = ri & (CHUNK - 1), ci & (CHUNK - 1)
    strict = jnp.logical_and(same_blk, tok_c < tok_r)
    incl = jnp.logical_and(same_blk, tok_c <= tok_r)
    eye = ri == ci
    rt_ = lax.broadcasted_iota(jnp.int32, (tblk, tblk), 0)
    ct_ = lax.broadcasted_iota(jnp.int32, (tblk, tblk), 1)
    same_chunk = (rt_ & -CHUNK) == (ct_ & -CHUNK)
    chunk_tril = jnp.where(jnp.logical_and(same_chunk, ct_ <= rt_), 1.0, 0.0).astype(BF16)

    def seg_sum(x):
        xb = x.astype(BF16)
        return jnp.concatenate(
            [_dot(xb[:, p * PAIR:(p + 1) * PAIR], head_ones) for p in range(n_pairs)], axis=1)

    def stack(x):
        return jnp.concatenate([jnp.where(lo, x, 0.0), jnp.where(lo, 0.0, x)], axis=0)

    r, k, v, lw, a = r_ref[...], k_ref[...], v_ref[...], lw_ref[...], a_ref[...]
    kk = k * kk_ref[...]
    kk = kk / jnp.maximum(jnp.sqrt(seg_sum(kk * kk)), 1e-12)
    k = k * (1.0 + (a - 1.0) * ka_ref[...])
    b = kk * a
    bonus = seg_sum(r * k * rk_ref[...]) * v

    lw_hi, lw_lo = _split2(lw)
    cum = _dot(chunk_tril, lw_hi) + _dot(chunk_tril, lw_lo)

    n_chunks = tblk // CHUNK
    units = [(p, c) for c in range(n_chunks) for p in range(n_pairs)]
    idx = range(len(units))
    n_levels = int(math.log2(CHUNK))
    at, rt, bt, kt, bh, kh, vs, dec = [], [], [], [], [], [], [], []
    for p, c in units:
        rows, cols = slice(c * CHUNK, (c + 1) * CHUNK), slice(p * PAIR, (p + 1) * PAIR)
        cu, lwu = cum[rows, cols], lw[rows, cols]
        ru, ku, bu = r[rows, cols], k[rows, cols], b[rows, cols]
        end = cu[CHUNK - 1:CHUNK, :]
        e_pos, e_neg = jnp.exp(cu), jnp.exp(-cu)
        e_prev, e_end = jnp.exp(cu - lwu), jnp.exp(end - cu)
        at.append(stack(-kk[rows, cols] * e_prev))
        rt.append(stack(ru * e_pos))
        bt.append(stack(bu * e_neg).astype(BF16))
        kt.append(stack(ku * e_neg).astype(BF16))
        bh.append(jnp.transpose(stack(bu * e_end)).astype(BF16))
        kh.append(jnp.transpose(stack(ku * e_end)).astype(BF16))
        vs.append(stack(v[rows, cols]).astype(BF16))
        dec.append(jnp.where(eye, jnp.exp(end), 0.0))

    lhs = [jnp.concatenate([at[u], rt[u]], axis=0).astype(BF16) for u in idx]
    gbk = [_dot_nt(lhs[u], jnp.concatenate([bt[u], kt[u]], axis=0)) for u in idx]
    gb = [gbk[u][:, :PAIR] for u in idx]
    gk = [gbk[u][:, PAIR:] for u in idx]
    a_kv = [jnp.concatenate([jnp.where(strict, gk[u][:PAIR], 0.0),
                             jnp.where(incl, gk[u][PAIR:], 0.0)], axis=0).astype(BF16) for u in idx]
    akv = [_dot(a_kv[u], vs[u]) for u in idx]
    khv = [_dot(kh[u], vs[u]) for u in idx]
    a_rb = [jnp.where(incl, gb[u][PAIR:], 0.0).astype(BF16) for u in idx]

    a_ab = [jnp.where(strict, gb[u][:PAIR], 0.0) for u in idx]
    inv = [jnp.where(eye, 1.0, a_ab[u]) for u in idx]
    pw = [a_ab[u].astype(BF16) for u in idx]
    pw = [_dot(pw[u], pw[u]).astype(BF16) for u in idx]
    for level in range(1, n_levels):
        if level + 1 < n_levels:
            both = [_dot(pw[u], jnp.concatenate([inv[u].astype(BF16), pw[u]], axis=1)) for u in idx]
            inv = [inv[u] + both[u][:, :PAIR] for u in idx]
            pw = [both[u][:, PAIR:].astype(BF16) for u in idx]
        else:
            inv = [inv[u] + _dot(pw[u], inv[u].astype(BF16)) for u in idx]
    xb = [_dot(inv[u].astype(BF16),
               jnp.concatenate([at[u], akv[u][:PAIR]], axis=1).astype(BF16)).astype(BF16) for u in idx]

    qy = [_dot(a_rb[u], xb[u]) + jnp.concatenate([rt[u], akv[u][PAIR:]], axis=1) for u in idx]
    mc = [_dot(bh[u], xb[u]) + jnp.concatenate([dec[u], khv[u]], axis=1) for u in idx]
    q_hat = [qy[u][:, :PAIR].astype(BF16) for u in idx]
    m_hat = [mc[u][:, :PAIR].astype(BF16) for u in idx]

    state = [s_ref[p] for p in range(n_pairs)]
    ys = [[None] * n_chunks for _ in range(n_pairs)]
    for u, (p, c) in enumerate(units):
        sb = state[p].astype(BF16)
        y = _dot(q_hat[u], sb) + qy[u][:, PAIR:]
        ys[p][c] = y[:CHUNK] + y[CHUNK:]
        state[p] = _dot(m_hat[u], sb) + mc[u][:, PAIR:]
    for p in range(n_pairs):
        s_ref[p] = state[p]

    y = jnp.concatenate([jnp.concatenate(ys[p], axis=0) for p in range(n_pairs)], axis=1)
    yh, yl = _split2(y)
    inv_n = 1.0 / RW_HEAD
    mu = (seg_sum(yh) + seg_sum(yl)) * inv_n
    d = y - mu
    var = seg_sum(d * d) * inv_n
    y = d * lax.rsqrt(var + GN_EPS) * gng_ref[...] + gnb_ref[...]
    o_ref[...] = ((y + bonus) * g_ref[...].astype(F32)).astype(BF16)


def _wkv7(r, k, v, lw, a, g, k_k, k_a, r_k, gn_g, gn_b, batch, seq_len):
    tok = pl.BlockSpec((None, WKV_TBLK, RW_WIDTH), lambda b, t: (b, t, 0))
    shape3 = (batch, seq_len, RW_WIDTH)
    params = (k_k, k_a, r_k, gn_g, gn_b)
    out = pl.pallas_call(
        _wkv_body,
        grid=(batch, seq_len // WKV_TBLK),
        in_specs=[tok] * 6 + [_resident(t.shape) for t in params],
        out_specs=tok,
        out_shape=jax.ShapeDtypeStruct(shape3, BF16),
        scratch_shapes=[pltpu.VMEM((N_PAIRS, PAIR, PAIR), F32)],
        compiler_params=_params("parallel", "arbitrary"),
        name="wkv7",
    )(*(t.reshape(shape3) for t in (r, k, v, lw, a, g)), *params)
    return out.reshape(batch * seq_len, RW_WIDTH)


def _mix_ln_body(attn_ref, rw_ref, gab_ref, x_ref, wba_ref, wbb_ref, wo_ref, g_ref, b_ref, o_ref):
    half = x_ref.shape[0] // 2
    for part in range(2):
        rows = slice(part * half, (part + 1) * half)
        gab = gab_ref[rows, :]
        merged = (gab[:, :D_MODEL].astype(F32) * _dot(attn_ref[rows, :], wba_ref[...])
                  + gab[:, D_MODEL:].astype(F32) * _dot(rw_ref[rows, :], wbb_ref[...]))
        mix = _dot(merged.astype(BF16), wo_ref[...])
        o_ref[rows, :] = _layer_norm(ALPHA * x_ref[rows, :] + mix, g_ref[...], b_ref[...])


def _mix_ln(attn, rw, gab, x, layer, wba, wbb, wo, g, b):
    m = x.shape[0]
    tile = lambda n: pl.BlockSpec((TM, n), lambda i: (i, 0))
    return pl.pallas_call(
        _mix_ln_body,
        grid=(m // TM,),
        in_specs=[tile(ATTN_WIDTH), tile(RW_WIDTH), tile(PW_GATES), tile(D_MODEL)]
        + [_resident(t.shape, (layer,)) for t in (wba, wbb, wo)] + [_resident(g.shape), _resident(b.shape)],
        out_specs=tile(D_MODEL),
        out_shape=jax.ShapeDtypeStruct((m, D_MODEL), F32),
        compiler_params=_params("parallel"),
        name="mix_ln",
    )(attn, rw, gab, x, wba, wbb, wo, g, b)


def _t5_bucket(n):
    max_exact = N_BUCKETS // 2
    nf = jnp.maximum(n, 1).astype(F32)
    large = max_exact + (jnp.log(nf / max_exact) / math.log(MAX_DISTANCE / max_exact)
                         * (N_BUCKETS - max_exact)).astype(jnp.int32)
    large = jnp.minimum(large, N_BUCKETS - 1)
    return jnp.where(n < max_exact, n, large)


def _attention_tables(rel_bias, sinks):
    dist_bias = rel_bias[_t5_bucket(jnp.arange(WINDOW, dtype=jnp.int32))].T
    period = 3 * BLOCK
    neg = lambda n: jnp.full((ATTN_Q_HEADS, n), -jnp.inf, F32)
    row0 = jnp.concatenate([neg(1), dist_bias[:, ::-1].astype(F32), neg(period - 1 - WINDOW)], axis=1)
    bias = jnp.tile(row0, (1, BLOCK))[:, :BLOCK * (period - 1)]
    bias = bias.reshape(ATTN_Q_HEADS, BLOCK, period - 1)[:, :, :2 * BLOCK]
    bias = bias.reshape(ATTN_KV_HEADS, ATTN_GROUP * BLOCK, 2 * BLOCK)
    sink = jnp.repeat(sinks.astype(F32), BLOCK).reshape(ATTN_KV_HEADS, ATTN_GROUP * BLOCK, 1)
    return bias, sink


def kernel(x, ffn_w_gu, ffn_w_down, ln_g, ln_b, w_in, b_in, rel_bias, attn_sinks, shift_mu,
           rw_w0, rw_w2, rw_a0, rw_a2, rw_g2, rw_k_k, rw_k_a, rw_r_k, rw_gn_g, rw_gn_b,
           rw_v0, rw_v1, rw_v2, w_branch_attn, w_branch_rwkv, w_out):
    batch, seq_len, _ = x.shape
    depth = w_in.shape[0]
    assert seq_len % TM == 0 and seq_len % TQ == 0 and seq_len % WKV_TBLK == 0
    row = lambda t: t.reshape(1, -1)
    bf = lambda t: t.astype(BF16)
    wgu, wd, w_proj = bf(ffn_w_gu), bf(ffn_w_down), bf(w_in)
    wba, wbb, wo = bf(w_branch_attn), bf(w_branch_rwkv), bf(w_out)
    h = x.reshape(batch * seq_len, D_MODEL)
    v_first = None
    for l in range(depth):
        h = _ffn_ln(h, wgu, wd, (l, 0), row(ln_g[l, 0]), row(ln_b[l, 0]))
        vmix = None if l == 0 else (v_first, row(rw_v0[l - 1]), bf(rw_v1[l - 1]), bf(rw_v2[l - 1]))
        gab, q, kd, vd, r, k, v, lw, a, g = _proj_prep(
            h, seq_len, l, w_proj, row(b_in[l]), row(shift_mu[l]), row(rw_w0[l]), bf(rw_w2[l]),
            row(rw_a0[l]), bf(rw_a2[l]), bf(rw_g2[l]), vmix)
        if l == 0:
            v_first = v
        bias, sink = _attention_tables(rel_bias, attn_sinks[l])
        attn = _swa(q, kd, vd, bias, sink, batch, seq_len)
        rw = _wkv7(r, k, v, lw, a, g, row(rw_k_k[l]), row(rw_k_a[l]), row(rw_r_k[l]),
                   row(rw_gn_g[l]), row(rw_gn_b[l]), batch, seq_len)
        h = _mix_ln(attn, rw, gab, h, l, wba, wbb, wo, row(ln_g[l, 1]), row(ln_b[l, 1]))
        h = _ffn_ln(h, wgu, wd, (l, 1), row(ln_g[l, 2]), row(ln_b[l, 2]))
    return h.reshape(batch, seq_len, D_MODEL)
```

```python
import functools
import math

import jax
import jax.numpy as jnp
from jax import lax
from jax.experimental import pallas as pl
from jax.experimental.pallas import tpu as pltpu

F32 = jnp.float32
BF16 = jnp.bfloat16

D_MODEL = 1024
DEPTH = 2
HEAD_DIM = 64
ATTN_Q_HEADS = 8
ATTN_KV_HEADS = 2
ATTN_GROUP = ATTN_Q_HEADS // ATTN_KV_HEADS
ATTN_WIDTH = ATTN_Q_HEADS * HEAD_DIM
KV_WIDTH = ATTN_KV_HEADS * HEAD_DIM
WINDOW = 128
BLOCK = 128
N_BUCKETS = 32
MAX_DISTANCE = 128
RW_HEADS = 8
RW_HEAD = 64
RW_WIDTH = RW_HEADS * RW_HEAD
DECAY_RANK = 64
A_RANK = 64
V_RANK = 32
G_RANK = 128
GN_EPS = 64e-5
D_FF = 2816
ALPHA = (2 * DEPTH) ** 0.25
LN_EPS = 1e-5
RW_COLS = 3 * RW_WIDTH + DECAY_RANK + A_RANK + G_RANK

LANES = 128
MXU_DIM = 256
VMEM_LIMIT = 56 * 1024 * 1024

TM = 512
TM_FFN = 1024
FF_CHUNK = MXU_DIM
N_FF_CHUNKS = D_FF // FF_CHUNK
TQ = 512
CHUNK = 64
WKV_TBLK = 256
PAIR = 2 * RW_HEAD
N_PAIRS = RW_WIDTH // PAIR

PW_GATES = 2 * D_MODEL
PW_Q = ATTN_WIDTH
PW_KD = 2 * KV_WIDTH
PW_VD = 2 * KV_WIDTH
PO_Q = PW_GATES
PO_RW = PO_Q + ATTN_WIDTH + 2 * KV_WIDTH
PROJ_COLS = PO_RW + RW_COLS


def _dot(a, b):
    return jnp.dot(a, b, preferred_element_type=F32)


def _dot_nt(a, b):
    return lax.dot_general(a, b, (((1,), (1,)), ((), ())), preferred_element_type=F32)


def _layer_norm(y, g, b):
    mu = jnp.mean(y, axis=-1, keepdims=True)
    d = y - mu
    var = jnp.mean(d * d, axis=-1, keepdims=True)
    return d * lax.rsqrt(var + LN_EPS) * g + b


def _resident(shape, lead=()):
    rest = len(shape) - len(lead)
    block = (None,) * len(lead) + tuple(shape[len(lead):])
    return pl.BlockSpec(block, lambda *_: tuple(lead) + (0,) * rest, pipeline_mode=pl.Buffered(1))


def _params(*sem):
    return pltpu.CompilerParams(dimension_semantics=sem, vmem_limit_bytes=VMEM_LIMIT)


def _ffn_ln_body(x_ref, wgu_ref, wd_ref, g_ref, b_ref, o_ref):
    x = x_ref[...]
    xb = x.astype(BF16)
    acc = jnp.zeros(x.shape, F32)
    for j in range(N_FF_CHUNKS):
        cols = slice(j * FF_CHUNK, (j + 1) * FF_CHUNK)
        gate = _dot(xb, wgu_ref[:, cols])
        up = _dot(xb, wgu_ref[:, D_FF + j * FF_CHUNK:D_FF + (j + 1) * FF_CHUNK])
        h = jax.nn.silu(gate) * up
        acc = acc + _dot(h.astype(BF16), wd_ref[cols, :])
    o_ref[...] = _layer_norm(ALPHA * x + 0.5 * acc, g_ref[...], b_ref[...])


def _ffn_ln(x, wgu, wd, lead, g, b):
    m = x.shape[0]
    return pl.pallas_call(
        _ffn_ln_body,
        grid=(m // TM_FFN,),
        in_specs=[pl.BlockSpec((TM_FFN, D_MODEL), lambda i: (i, 0)),
                  _resident(wgu.shape, lead), _resident(wd.shape, lead),
                  _resident(g.shape), _resident(b.shape)],
        out_specs=pl.BlockSpec((TM_FFN, D_MODEL), lambda i: (i, 0)),
        out_shape=jax.ShapeDtypeStruct((m, D_MODEL), F32),
        compiler_params=_params("parallel"),
        name="ffn_ln",
    )(x, wgu, wd, g, b)


def _softplus(z):
    return jnp.maximum(z, 0.0) + jnp.log1p(jnp.exp(-jnp.abs(z)))


def _proj_body(tiles_per_seq, has_vmix, *refs):
    (x_ref, w_ref, b_ref, mu_ref, w0_ref, w2_ref, a0_ref, a2_ref, g2_ref) = refs[:9]
    refs = refs[9:]
    if has_vmix:
        vf_ref, v0_ref, v1_ref, v2_ref = refs[:4]
        refs = refs[4:]
    (gab_ref, q_ref, kd_ref, vd_ref, r_ref, k_ref, v_ref, lw_ref, a_ref, g_ref, carry_ref) = refs

    i = pl.program_id(0)
    tm = x_ref.shape[0]

    @pl.when(i % tiles_per_seq == 0)
    def _():
        carry_ref[...] = jnp.zeros(carry_ref.shape, F32)

    xb = x_ref[...].astype(BF16)
    part = lambda lo, hi: _dot(xb, w_ref[:, lo:hi]) + b_ref[:, lo:hi]
    u = part(PO_RW, PROJ_COLS)
    gates = part(0, PW_GATES)

    before_tile = carry_ref[7:8, :]
    carry_ref[...] = u[tm - 8:, :]
    row = lax.broadcasted_iota(jnp.int32, (tm, 1), 0)
    u_prev = jnp.where(row == 0, before_tile, pltpu.roll(u, 1, 0))
    u = u + (u_prev - u) * mu_ref[...]

    c = RW_WIDTH
    r = u[:, :c]
    k = u[:, c:2 * c]
    v = u[:, 2 * c:3 * c]
    o = 3 * c
    xw = u[:, o:o + DECAY_RANK]
    xa = u[:, o + DECAY_RANK:o + DECAY_RANK + A_RANK]
    xg = u[:, o + DECAY_RANK + A_RANK:]

    w = -_softplus(-(w0_ref[...] + _dot(jnp.tanh(xw).astype(BF16), w2_ref[...]))) - 0.5
    lw_ref[...] = -jnp.exp(w)
    a_ref[...] = jax.nn.sigmoid(a0_ref[...] + _dot(xa.astype(BF16), a2_ref[...]))
    g_ref[...] = _dot(jax.nn.sigmoid(xg).astype(BF16), g2_ref[...]).astype(BF16)
    if has_vmix:
        low = _dot(v.astype(BF16), v1_ref[...])
        mix = jax.nn.sigmoid(v0_ref[...] + _dot(low.astype(BF16), v2_ref[...]))
        v = v + (vf_ref[...] - v) * mix
    r_ref[...] = r
    k_ref[...] = k
    v_ref[...] = v

    qkv = part(PO_Q, PO_RW)
    gab_ref[...] = jax.nn.sigmoid(gates).astype(BF16)
    q_ref[...] = (qkv[:, :ATTN_WIDTH] * (HEAD_DIM ** -0.5)).astype(BF16)
    lo = lax.broadcasted_iota(jnp.int32, (1, LANES), 1) < HEAD_DIM
    for ref, at in ((kd_ref, ATTN_WIDTH), (vd_ref, ATTN_WIDTH + KV_WIDTH)):
        pair = qkv[:, at:at + KV_WIDTH]
        swapped = pltpu.roll(pair, HEAD_DIM, 1)
        ref[...] = jnp.concatenate([jnp.where(lo, pair, swapped),
                                    jnp.where(lo, swapped, pair)], axis=1).astype(BF16)


def _proj_prep(x, seq_len, layer, w, b, mu, w0, w2, a0, a2, g2, vmix):
    m = x.shape[0]
    tile = lambda n: pl.BlockSpec((TM, n), lambda i: (i, 0))
    has_vmix = vmix is not None
    args = [x, w, b, mu, w0, w2, a0, a2, g2]
    in_specs = [tile(D_MODEL), _resident(w.shape, (layer,))] + [_resident(t.shape) for t in args[2:]]
    if has_vmix:
        vf, v0, v1, v2 = vmix
        args += [vf, v0, v1, v2]
        in_specs += [tile(RW_WIDTH), _resident(v0.shape), _resident(v1.shape), _resident(v2.shape)]
    widths = [(PW_GATES, BF16), (PW_Q, BF16), (PW_KD, BF16), (PW_VD, BF16)] \
        + [(RW_WIDTH, F32)] * 5 + [(RW_WIDTH, BF16)]
    return pl.pallas_call(
        functools.partial(_proj_body, seq_len // TM, has_vmix),
        grid=(m // TM,),
        in_specs=in_specs,
        out_specs=[tile(n) for n, _ in widths],
        out_shape=[jax.ShapeDtypeStruct((m, n), dt) for n, dt in widths],
        scratch_shapes=[pltpu.VMEM((8, RW_COLS), F32)],
        compiler_params=_params("arbitrary"),
        name="proj_prep",
    )(*args)


def _swa_body(q_ref, kc_ref, kp_ref, vc_ref, vp_ref, bias_ref, sink_ref, o_ref):
    i = pl.program_id(1)
    lane = lax.broadcasted_iota(jnp.int32, (1, LANES), 1)
    lo = lane < HEAD_DIM
    key = lax.broadcasted_iota(jnp.int32, (1, 2 * BLOCK), 1)
    no_prev = jnp.logical_and(i == 0, key < BLOCK)
    zero = jnp.zeros((), BF16)
    units = [(j, h) for j in range(TQ // BLOCK) for h in range(ATTN_KV_HEADS)]
    idx = range(len(units))
    scores, v_lo, v_hi = [], [], []
    for j, h in units:
        rows = slice(j * BLOCK, (j + 1) * BLOCK)
        cols = slice(h * LANES, (h + 1) * LANES)
        if j == 0:
            k_prev, v_prev = kp_ref[:, cols], vp_ref[:, cols]
        else:
            before = slice((j - 1) * BLOCK, j * BLOCK)
            k_prev, v_prev = kc_ref[before, cols], vc_ref[before, cols]
        kb = jnp.concatenate([k_prev, kc_ref[rows, cols]], axis=0)
        vb = jnp.concatenate([v_prev, vc_ref[rows, cols]], axis=0)
        parts = []
        for p in (2 * h, 2 * h + 1):
            qp = q_ref[rows, p * LANES:(p + 1) * LANES]
            parts += [jnp.where(lo, qp, zero), jnp.where(lo, zero, qp)]
        s = _dot_nt(jnp.concatenate(parts, axis=0), kb) + bias_ref[h]
        if j == 0:
            s = jnp.where(no_prev, -jnp.inf, s)
        scores.append(s)
        v_lo.append(jnp.where(lo, vb, zero))
        v_hi.append(jnp.where(lo, zero, vb))
    sinks = [sink_ref[h] for _, h in units]
    mx = [jnp.maximum(jnp.max(scores[u], axis=-1, keepdims=True), sinks[u]) for u in idx]
    e = [jnp.exp(scores[u] - mx[u]) for u in idx]
    den = [jnp.sum(e[u], axis=-1, keepdims=True) + jnp.exp(sinks[u] - mx[u]) for u in idx]
    eb = [e[u].astype(BF16) for u in idx]
    for u, (j, h) in enumerate(units):
        rows = slice(j * BLOCK, (j + 1) * BLOCK)
        for n, p in enumerate((2 * h, 2 * h + 1)):
            top = slice(2 * n * BLOCK, (2 * n + 1) * BLOCK)
            bot = slice((2 * n + 1) * BLOCK, (2 * n + 2) * BLOCK)
            out = (_dot(eb[u][top], v_lo[u]) / den[u][top]
                   + _dot(eb[u][bot], v_hi[u]) / den[u][bot])
            o_ref[rows, p * LANES:(p + 1) * LANES] = out.astype(BF16)


def _swa(q, kd, vd, bias, sink, batch, seq_len):
    q3 = q.reshape(batch, seq_len, PW_Q)
    kd3 = kd.reshape(batch, seq_len, PW_KD)
    vd3 = vd.reshape(batch, seq_len, PW_VD)
    cur = lambda n: pl.BlockSpec((None, TQ, n), lambda b, i: (b, i, 0))
    prev = lambda n: pl.BlockSpec(
        (None, BLOCK, n), lambda b, i: (b, jnp.maximum(i * (TQ // BLOCK) - 1, 0), 0))
    out = pl.pallas_call(
        _swa_body,
        grid=(batch, seq_len // TQ),
        in_specs=[cur(PW_Q), cur(PW_KD), prev(PW_KD), cur(PW_VD), prev(PW_VD),
                  _resident(bias.shape), _resident(sink.shape)],
        out_specs=cur(ATTN_WIDTH),
        out_shape=jax.ShapeDtypeStruct((batch, seq_len, ATTN_WIDTH), BF16),
        compiler_params=_params("parallel", "parallel"),
        name="swa",
    )(q3, kd3, kd3, vd3, vd3, bias, sink)
    return out.reshape(batch * seq_len, ATTN_WIDTH)


def _split2(x):
    hi = x.astype(BF16)
    return hi, (x - hi.astype(F32)).astype(BF16)


def _wkv_body(r_ref, k_ref, v_ref, lw_ref, a_ref, g_ref,
              kk_ref, ka_ref, rk_ref, gng_ref, gnb_ref, o_ref, s_ref):
    t = pl.program_id(1)
    tblk, width = r_ref.shape
    n_pairs = width // PAIR

    @pl.when(t == 0)
    def _():
        s_ref[...] = jnp.zeros(s_ref.shape, F32)

    lane = lax.broadcasted_iota(jnp.int32, (1, PAIR), 1)
    lo = lane < RW_HEAD
    ri = lax.broadcasted_iota(jnp.int32, (PAIR, PAIR), 0)
    ci = lax.broadcasted_iota(jnp.int32, (PAIR, PAIR), 1)
    same_head = (ri >= RW_HEAD) == (ci >= RW_HEAD)
    head_ones = jnp.where(same_head, 1.0, 0.0).astype(BF16)
    eye = ri == ci
    tok_r = lax.broadcasted_iota(jnp.int32, (CHUNK, PAIR), 0)
    tok_c = lax.broadcasted_iota(jnp.int32, (CHUNK, PAIR), 1) & (CHUNK - 1)
    strict, incl, diag = tok_c < tok_r, tok_c <= tok_r, tok_c == tok_r
    rt_ = lax.broadcasted_iota(jnp.int32, (tblk, tblk), 0)
    ct_ = lax.broadcasted_iota(jnp.int32, (tblk, tblk), 1)
    same_chunk = (rt_ & -CHUNK) == (ct_ & -CHUNK)
    chunk_tril = jnp.where(jnp.logical_and(same_chunk, ct_ <= rt_), 1.0, 0.0).astype(BF16)

    def seg_sum(x):
        xb = x.astype(BF16)
        return jnp.concatenate(
            [_dot(xb[:, p * PAIR:(p + 1) * PAIR], head_ones) for p in range(n_pairs)], axis=1)

    def stack(x):
        x = x.astype(BF16)
        zero = jnp.zeros((), BF16)
        return jnp.concatenate([jnp.where(lo, x, zero), jnp.where(lo, zero, x)], axis=0)

    def per_head(x):
        return jnp.where(jnp.concatenate([same_head] * (x.shape[1] // PAIR), axis=1), x, 0.0)

    r, k, v, lw, a = r_ref[...], k_ref[...], v_ref[...], lw_ref[...], a_ref[...]
    kk = k * kk_ref[...]
    kk = kk / jnp.maximum(jnp.sqrt(seg_sum(kk * kk)), 1e-12)
    k = k * (1.0 + (a - 1.0) * ka_ref[...])
    b = kk * a
    bonus = seg_sum(r * k * rk_ref[...]) * v

    lw_hi, lw_lo = _split2(lw)
    cum = _dot(chunk_tril, lw_hi) + _dot(chunk_tril, lw_lo)

    n_chunks = tblk // CHUNK
    units = [(p, c) for c in range(n_chunks) for p in range(n_pairs)]
    idx = range(len(units))
    n_levels = int(math.log2(CHUNK))
    at, rt, lhs, rhs, bh, kh, vd, vs, dec = [], [], [], [], [], [], [], [], []
    for p, c in units:
        rows, cols = slice(c * CHUNK, (c + 1) * CHUNK), slice(p * PAIR, (p + 1) * PAIR)
        cu, lwu = cum[rows, cols], lw[rows, cols]
        ru, ku, bu, vu = r[rows, cols], k[rows, cols], b[rows, cols], v[rows, cols]
        end = cu[CHUNK - 1:CHUNK, :]
        e_pos, e_neg = jnp.exp(cu), jnp.exp(-cu)
        e_prev, e_end = jnp.exp(cu - lwu), jnp.exp(end - cu)
        at.append(-kk[rows, cols] * e_prev)
        rt.append(ru * e_pos)
        lhs.append(jnp.concatenate([at[-1], rt[-1]], axis=0).astype(BF16))
        rhs.append(jnp.concatenate([stack(bu * e_neg), stack(ku * e_neg)], axis=0))
        bh.append(jnp.transpose(bu * e_end).astype(BF16))
        kh.append(jnp.transpose(ku * e_end).astype(BF16))
        vd.append(vu.astype(BF16))
        vs.append(stack(vu))
        dec.append(jnp.where(eye, jnp.exp(end), 0.0))

    g = [_dot_nt(lhs[u], rhs[u]) for u in idx]
    a_ab = [jnp.where(strict, g[u][:CHUNK, :PAIR], 0.0) for u in idx]
    a_rb = [jnp.where(incl, g[u][CHUNK:, :PAIR], 0.0).astype(BF16) for u in idx]
    a_kv = [jnp.concatenate([jnp.where(strict, g[u][:CHUNK, PAIR:], 0.0),
                             jnp.where(incl, g[u][CHUNK:, PAIR:], 0.0)], axis=0).astype(BF16) for u in idx]
    akv = [_dot(a_kv[u], vs[u]) for u in idx]
    khv = [_dot(kh[u], vd[u]) for u in idx]

    inv = [jnp.where(diag, 1.0, a_ab[u]) for u in idx]
    pw = [a_ab[u].astype(BF16) for u in idx]
    pw = [_dot(pw[u], stack(pw[u])).astype(BF16) for u in idx]
    for level in range(1, n_levels):
        if level + 1 < n_levels:
            both = [_dot(pw[u], jnp.concatenate([stack(inv[u]), stack(pw[u])], axis=1)) for u in idx]
            inv = [inv[u] + both[u][:, :PAIR] for u in idx]
            pw = [both[u][:, PAIR:].astype(BF16) for u in idx]
        else:
            inv = [inv[u] + _dot(pw[u], stack(inv[u])) for u in idx]
    x = [_dot(inv[u].astype(BF16),
              jnp.concatenate([stack(at[u]), stack(akv[u][:CHUNK])], axis=1)).astype(BF16) for u in idx]

    qy = [_dot(a_rb[u], jnp.concatenate([stack(x[u][:, :PAIR]), stack(x[u][:, PAIR:])], axis=1))
          + jnp.concatenate([rt[u], akv[u][CHUNK:]], axis=1) for u in idx]
    mc = [per_head(_dot(bh[u], x[u])) + jnp.concatenate([dec[u], per_head(khv[u])], axis=1)
          for u in idx]
    q_hat = [qy[u][:, :PAIR].astype(BF16) for u in idx]
    m_hat = [mc[u][:, :PAIR].astype(BF16) for u in idx]

    state = [s_ref[p] for p in range(n_pairs)]
    ys = [[None] * n_chunks for _ in range(n_pairs)]
    for u, (p, c) in enumerate(units):
        sb = state[p].astype(BF16)
        ys[p][c] = _dot(q_hat[u], sb) + qy[u][:, PAIR:]
        state[p] = _dot(m_hat[u], sb) + mc[u][:, PAIR:]
    for p in range(n_pairs):
        s_ref[p] = state[p]

    y = jnp.concatenate([jnp.concatenate(ys[p], axis=0) for p in range(n_pairs)], axis=1)
    inv_n = 1.0 / RW_HEAD
    d = y - seg_sum(y) * inv_n
    var = seg_sum(d * d) * inv_n
    y = d * lax.rsqrt(var + GN_EPS) * gng_ref[...] + gnb_ref[...]
    o_ref[...] = ((y + bonus) * g_ref[...].astype(F32)).astype(BF16)


def _wkv7(r, k, v, lw, a, g, k_k, k_a, r_k, gn_g, gn_b, batch, seq_len):
    tok = pl.BlockSpec((None, WKV_TBLK, RW_WIDTH), lambda b, t: (b, t, 0))
    shape3 = (batch, seq_len, RW_WIDTH)
    params = (k_k, k_a, r_k, gn_g, gn_b)
    out = pl.pallas_call(
        _wkv_body,
        grid=(batch, seq_len // WKV_TBLK),
        in_specs=[tok] * 6 + [_resident(t.shape) for t in params],
        out_specs=tok,
        out_shape=jax.ShapeDtypeStruct(shape3, BF16),
        scratch_shapes=[pltpu.VMEM((N_PAIRS, PAIR, PAIR), F32)],
        compiler_params=_params("parallel", "arbitrary"),
        name="wkv7",
    )(*(t.reshape(shape3) for t in (r, k, v, lw, a, g)), *params)
    return out.reshape(batch * seq_len, RW_WIDTH)


def _mix_ln_body(attn_ref, rw_ref, gab_ref, x_ref, wba_ref, wbb_ref, wo_ref, g_ref, b_ref, o_ref):
    half = x_ref.shape[0] // 2
    for part in range(2):
        rows = slice(part * half, (part + 1) * half)
        gab = gab_ref[rows, :]
        merged = (gab[:, :D_MODEL].astype(F32) * _dot(attn_ref[rows, :], wba_ref[...])
                  + gab[:, D_MODEL:].astype(F32) * _dot(rw_ref[rows, :], wbb_ref[...]))
        mix = _dot(merged.astype(BF16), wo_ref[...])
        o_ref[rows, :] = _layer_norm(ALPHA * x_ref[rows, :] + mix, g_ref[...], b_ref[...])


def _mix_ln(attn, rw, gab, x, layer, wba, wbb, wo, g, b):
    m = x.shape[0]
    tile = lambda n: pl.BlockSpec((TM, n), lambda i: (i, 0))
    return pl.pallas_call(
        _mix_ln_body,
        grid=(m // TM,),
        in_specs=[tile(ATTN_WIDTH), tile(RW_WIDTH), tile(PW_GATES), tile(D_MODEL)]
        + [_resident(t.shape, (layer,)) for t in (wba, wbb, wo)] + [_resident(g.shape), _resident(b.shape)],
        out_specs=tile(D_MODEL),
        out_shape=jax.ShapeDtypeStruct((m, D_MODEL), F32),
        compiler_params=_params("parallel"),
        name="mix_ln",
    )(attn, rw, gab, x, wba, wbb, wo, g, b)


def _t5_bucket(n):
    max_exact = N_BUCKETS // 2
    nf = jnp.maximum(n, 1).astype(F32)
    large = max_exact + (jnp.log(nf / max_exact) / math.log(MAX_DISTANCE / max_exact)
                         * (N_BUCKETS - max_exact)).astype(jnp.int32)
    large = jnp.minimum(large, N_BUCKETS - 1)
    return jnp.where(n < max_exact, n, large)


def _attention_tables(rel_bias, sinks):
    dist_bias = rel_bias[_t5_bucket(jnp.arange(WINDOW, dtype=jnp.int32))].T
    period = 3 * BLOCK
    neg = lambda n: jnp.full((ATTN_Q_HEADS, n), -jnp.inf, F32)
    row0 = jnp.concatenate([neg(1), dist_bias[:, ::-1].astype(F32), neg(period - 1 - WINDOW)], axis=1)
    bias = jnp.tile(row0, (1, BLOCK))[:, :BLOCK * (period - 1)]
    bias = bias.reshape(ATTN_Q_HEADS, BLOCK, period - 1)[:, :, :2 * BLOCK]
    bias = bias.reshape(ATTN_KV_HEADS, ATTN_GROUP * BLOCK, 2 * BLOCK)
    sink = jnp.repeat(sinks.astype(F32), BLOCK).reshape(ATTN_KV_HEADS, ATTN_GROUP * BLOCK, 1)
    return bias, sink


def kernel(x, ffn_w_gu, ffn_w_down, ln_g, ln_b, w_in, b_in, rel_bias, attn_sinks, shift_mu,
           rw_w0, rw_w2, rw_a0, rw_a2, rw_g2, rw_k_k, rw_k_a, rw_r_k, rw_gn_g, rw_gn_b,
           rw_v0, rw_v1, rw_v2, w_branch_attn, w_branch_rwkv, w_out):
    batch, seq_len, _ = x.shape
    depth = w_in.shape[0]
    assert seq_len % TM == 0 and seq_len % TM_FFN == 0 and seq_len % TQ == 0 and seq_len % WKV_TBLK == 0
    row = lambda t: t.reshape(1, -1)
    bf = lambda t: t.astype(BF16)
    wgu, wd, w_proj = bf(ffn_w_gu), bf(ffn_w_down), bf(w_in)
    wba, wbb, wo = bf(w_branch_attn), bf(w_branch_rwkv), bf(w_out)
    h = x.reshape(batch * seq_len, D_MODEL)
    v_first = None
    for l in range(depth):
        h = _ffn_ln(h, wgu, wd, (l, 0), row(ln_g[l, 0]), row(ln_b[l, 0]))
        vmix = None if l == 0 else (v_first, row(rw_v0[l - 1]), bf(rw_v1[l - 1]), bf(rw_v2[l - 1]))
        gab, q, kd, vd, r, k, v, lw, a, g = _proj_prep(
            h, seq_len, l, w_proj, row(b_in[l]), row(shift_mu[l]), row(rw_w0[l]), bf(rw_w2[l]),
            row(rw_a0[l]), bf(rw_a2[l]), bf(rw_g2[l]), vmix)
        if l == 0:
            v_first = v
        bias, sink = _attention_tables(rel_bias, attn_sinks[l])
        attn = _swa(q, kd, vd, bias, sink, batch, seq_len)
        rw = _wkv7(r, k, v, lw, a, g, row(rw_k_k[l]), row(rw_k_a[l]), row(rw_r_k[l]),
                   row(rw_gn_g[l]), row(rw_gn_b[l]), batch, seq_len)
        h = _mix_ln(attn, rw, gab, h, l, wba, wbb, wo, row(ln_g[l, 1]), row(ln_b[l, 1]))
        h = _ffn_ln(h, wgu, wd, (l, 1), row(ln_g[l, 2]), row(ln_b[l, 2]))
    return h.reshape(batch, seq_len, D_MODEL)
```

```python
import functools
import math

import jax
import jax.numpy as jnp
from jax import lax
from jax.experimental import pallas as pl
from jax.experimental.pallas import tpu as pltpu

F32 = jnp.float32
BF16 = jnp.bfloat16

D_MODEL = 1024
DEPTH = 2
HEAD_DIM = 64
ATTN_Q_HEADS = 8
ATTN_KV_HEADS = 2
ATTN_GROUP = ATTN_Q_HEADS // ATTN_KV_HEADS
ATTN_WIDTH = ATTN_Q_HEADS * HEAD_DIM
KV_WIDTH = ATTN_KV_HEADS * HEAD_DIM
WINDOW = 128
BLOCK = 128
N_BUCKETS = 32
MAX_DISTANCE = 128
RW_HEADS = 8
RW_HEAD = 64
RW_WIDTH = RW_HEADS * RW_HEAD
DECAY_RANK = 64
A_RANK = 64
V_RANK = 32
G_RANK = 128
GN_EPS = 64e-5
D_FF = 2816
ALPHA = (2 * DEPTH) ** 0.25
LN_EPS = 1e-5
RW_COLS = 3 * RW_WIDTH + DECAY_RANK + A_RANK + G_RANK

LANES = 128
MXU_DIM = 256
VMEM_LIMIT = 56 * 1024 * 1024

TM = 512
TM_FFN = 1024
FF_CHUNK = MXU_DIM
N_FF_CHUNKS = D_FF // FF_CHUNK
TQ = 512
CHUNK = 64
WKV_TBLK = 256
PAIR = 2 * RW_HEAD
N_PAIRS = RW_WIDTH // PAIR

PW_GATES = 2 * D_MODEL
PW_Q = ATTN_WIDTH
PW_KD = 2 * KV_WIDTH
PW_VD = 2 * KV_WIDTH
PO_Q = PW_GATES
PO_RW = PO_Q + ATTN_WIDTH + 2 * KV_WIDTH
PROJ_COLS = PO_RW + RW_COLS


def _dot(a, b):
    return jnp.dot(a, b, preferred_element_type=F32)


def _dot_nt(a, b):
    return lax.dot_general(a, b, (((1,), (1,)), ((), ())), preferred_element_type=F32)


def _layer_norm(y, g, b):
    mu = jnp.mean(y, axis=-1, keepdims=True)
    d = y - mu
    var = jnp.mean(d * d, axis=-1, keepdims=True)
    return d * lax.rsqrt(var + LN_EPS) * g + b


def _resident(shape, lead=()):
    rest = len(shape) - len(lead)
    block = (None,) * len(lead) + tuple(shape[len(lead):])
    return pl.BlockSpec(block, lambda *_: tuple(lead) + (0,) * rest, pipeline_mode=pl.Buffered(1))


def _params(*sem):
    return pltpu.CompilerParams(dimension_semantics=sem, vmem_limit_bytes=VMEM_LIMIT)


def _ffn_ln_body(x_ref, wgu_ref, wd_ref, g_ref, b_ref, o_ref):
    x = x_ref[...]
    xb = x.astype(BF16)
    acc = jnp.zeros(x.shape, F32)
    for j in range(N_FF_CHUNKS):
        cols = slice(j * FF_CHUNK, (j + 1) * FF_CHUNK)
        gate = _dot(xb, wgu_ref[:, cols])
        up = _dot(xb, wgu_ref[:, D_FF + j * FF_CHUNK:D_FF + (j + 1) * FF_CHUNK])
        h = jax.nn.silu(gate) * up
        acc = acc + _dot(h.astype(BF16), wd_ref[cols, :])
    o_ref[...] = _layer_norm(ALPHA * x + 0.5 * acc, g_ref[...], b_ref[...])


def _ffn_ln(x, wgu, wd, lead, g, b):
    m = x.shape[0]
    return pl.pallas_call(
        _ffn_ln_body,
        grid=(m // TM_FFN,),
        in_specs=[pl.BlockSpec((TM_FFN, D_MODEL), lambda i: (i, 0)),
                  _resident(wgu.shape, lead), _resident(wd.shape, lead),
                  _resident(g.shape), _resident(b.shape)],
        out_specs=pl.BlockSpec((TM_FFN, D_MODEL), lambda i: (i, 0)),
        out_shape=jax.ShapeDtypeStruct((m, D_MODEL), F32),
        compiler_params=_params("parallel"),
        name="ffn_ln",
    )(x, wgu, wd, g, b)


def _softplus(z):
    return jnp.maximum(z, 0.0) + jnp.log1p(jnp.exp(-jnp.abs(z)))


def _proj_body(tiles_per_seq, has_vmix, *refs):
    (x_ref, w_ref, b_ref, mu_ref, w0_ref, w2_ref, a0_ref, a2_ref, g2_ref) = refs[:9]
    refs = refs[9:]
    if has_vmix:
        vf_ref, v0_ref, v1_ref, v2_ref = refs[:4]
        refs = refs[4:]
    (gab_ref, q_ref, kd_ref, vd_ref, r_ref, k_ref, v_ref, lw_ref, a_ref, g_ref, carry_ref) = refs

    i = pl.program_id(0)
    tm = x_ref.shape[0]

    @pl.when(i % tiles_per_seq == 0)
    def _():
        carry_ref[...] = jnp.zeros(carry_ref.shape, F32)

    xb = x_ref[...].astype(BF16)
    part = lambda lo, hi: _dot(xb, w_ref[:, lo:hi]) + b_ref[:, lo:hi]
    u = part(PO_RW, PROJ_COLS)
    gates = part(0, PW_GATES)

    before_tile = carry_ref[7:8, :]
    carry_ref[...] = u[tm - 8:, :]
    row = lax.broadcasted_iota(jnp.int32, (tm, 1), 0)
    u_prev = jnp.where(row == 0, before_tile, pltpu.roll(u, 1, 0))
    u = u + (u_prev - u) * mu_ref[...]

    c = RW_WIDTH
    r = u[:, :c]
    k = u[:, c:2 * c]
    v = u[:, 2 * c:3 * c]
    o = 3 * c
    xw = u[:, o:o + DECAY_RANK]
    xa = u[:, o + DECAY_RANK:o + DECAY_RANK + A_RANK]
    xg = u[:, o + DECAY_RANK + A_RANK:]

    w = -_softplus(-(w0_ref[...] + _dot(jnp.tanh(xw).astype(BF16), w2_ref[...]))) - 0.5
    lw_ref[...] = -jnp.exp(w)
    a_ref[...] = jax.nn.sigmoid(a0_ref[...] + _dot(xa.astype(BF16), a2_ref[...]))
    g_ref[...] = _dot(jax.nn.sigmoid(xg).astype(BF16), g2_ref[...]).astype(BF16)
    if has_vmix:
        low = _dot(v.astype(BF16), v1_ref[...])
        mix = jax.nn.sigmoid(v0_ref[...] + _dot(low.astype(BF16), v2_ref[...]))
        v = v + (vf_ref[...] - v) * mix
    r_ref[...] = r
    k_ref[...] = k
    v_ref[...] = v

    qkv = part(PO_Q, PO_RW)
    gab_ref[...] = jax.nn.sigmoid(gates).astype(BF16)
    q_ref[...] = (qkv[:, :ATTN_WIDTH] * (HEAD_DIM ** -0.5)).astype(BF16)
    lo = lax.broadcasted_iota(jnp.int32, (1, LANES), 1) < HEAD_DIM
    for ref, at in ((kd_ref, ATTN_WIDTH), (vd_ref, ATTN_WIDTH + KV_WIDTH)):
        pair = qkv[:, at:at + KV_WIDTH]
        swapped = pltpu.roll(pair, HEAD_DIM, 1)
        ref[...] = jnp.concatenate([jnp.where(lo, pair, swapped),
                                    jnp.where(lo, swapped, pair)], axis=1).astype(BF16)


def _proj_prep(x, seq_len, layer, w, b, mu, w0, w2, a0, a2, g2, vmix):
    m = x.shape[0]
    tile = lambda n: pl.BlockSpec((TM, n), lambda i: (i, 0))
    has_vmix = vmix is not None
    args = [x, w, b, mu, w0, w2, a0, a2, g2]
    in_specs = [tile(D_MODEL), _resident(w.shape, (layer,))] + [_resident(t.shape) for t in args[2:]]
    if has_vmix:
        vf, v0, v1, v2 = vmix
        args += [vf, v0, v1, v2]
        in_specs += [tile(RW_WIDTH), _resident(v0.shape), _resident(v1.shape), _resident(v2.shape)]
    widths = [(PW_GATES, BF16), (PW_Q, BF16), (PW_KD, BF16), (PW_VD, BF16)] \
        + [(RW_WIDTH, F32)] * 5 + [(RW_WIDTH, BF16)]
    return pl.pallas_call(
        functools.partial(_proj_body, seq_len // TM, has_vmix),
        grid=(m // TM,),
        in_specs=in_specs,
        out_specs=[tile(n) for n, _ in widths],
        out_shape=[jax.ShapeDtypeStruct((m, n), dt) for n, dt in widths],
        scratch_shapes=[pltpu.VMEM((8, RW_COLS), F32)],
        compiler_params=_params("arbitrary"),
        name="proj_prep",
    )(*args)


def _swa_body(q_ref, kc_ref, kp_ref, vc_ref, vp_ref, bias_ref, o_ref):
    i = pl.program_id(1)
    lane = lax.broadcasted_iota(jnp.int32, (1, LANES), 1)
    lo = lane < HEAD_DIM
    key = lax.broadcasted_iota(jnp.int32, (1, 2 * BLOCK), 1)
    not_key0 = lax.broadcasted_iota(jnp.int32, (2 * BLOCK, 1), 0) > 0
    no_prev = jnp.logical_and(i == 0, jnp.logical_and(key < BLOCK, key > 0))
    ones = jnp.ones((2 * BLOCK, LANES), BF16)
    zero = jnp.zeros((), BF16)
    units = [(j, h) for j in range(TQ // BLOCK) for h in range(ATTN_KV_HEADS)]
    idx = range(len(units))
    scores, v_lo, v_hi = [], [], []
    for j, h in units:
        rows = slice(j * BLOCK, (j + 1) * BLOCK)
        cols = slice(h * LANES, (h + 1) * LANES)
        if j == 0:
            k_prev, v_prev = kp_ref[:, cols], vp_ref[:, cols]
        else:
            before = slice((j - 1) * BLOCK, j * BLOCK)
            k_prev, v_prev = kc_ref[before, cols], vc_ref[before, cols]
        kb = jnp.concatenate([k_prev, kc_ref[rows, cols]], axis=0)
        vb = jnp.concatenate([v_prev, vc_ref[rows, cols]], axis=0)
        kb, vb = jnp.where(not_key0, kb, zero), jnp.where(not_key0, vb, zero)
        parts = []
        for p in (2 * h, 2 * h + 1):
            qp = q_ref[rows, p * LANES:(p + 1) * LANES]
            parts += [jnp.where(lo, qp, zero), jnp.where(lo, zero, qp)]
        s = _dot_nt(jnp.concatenate(parts, axis=0), kb) + bias_ref[h]
        if j == 0:
            s = jnp.where(no_prev, -jnp.inf, s)
        scores.append(s)
        v_lo.append(jnp.where(lo, vb, zero))
        v_hi.append(jnp.where(lo, zero, vb))
    mx = [jnp.max(scores[u], axis=-1, keepdims=True) for u in idx]
    eb = [jnp.exp(scores[u] - mx[u]).astype(BF16) for u in idx]
    den = [_dot(eb[u], ones) for u in idx]
    for u, (j, h) in enumerate(units):
        rows = slice(j * BLOCK, (j + 1) * BLOCK)
        for n, p in enumerate((2 * h, 2 * h + 1)):
            top = slice(2 * n * BLOCK, (2 * n + 1) * BLOCK)
            bot = slice((2 * n + 1) * BLOCK, (2 * n + 2) * BLOCK)
            out = (_dot(eb[u][top], v_lo[u]) / den[u][top]
                   + _dot(eb[u][bot], v_hi[u]) / den[u][bot])
            o_ref[rows, p * LANES:(p + 1) * LANES] = out.astype(BF16)


def _swa(q, kd, vd, bias, layer, batch, seq_len):
    q3 = q.reshape(batch, seq_len, PW_Q)
    kd3 = kd.reshape(batch, seq_len, PW_KD)
    vd3 = vd.reshape(batch, seq_len, PW_VD)
    cur = lambda n: pl.BlockSpec((None, TQ, n), lambda b, i: (b, i, 0))
    prev = lambda n: pl.BlockSpec(
        (None, BLOCK, n), lambda b, i: (b, jnp.maximum(i * (TQ // BLOCK) - 1, 0), 0))
    out = pl.pallas_call(
        _swa_body,
        grid=(batch, seq_len // TQ),
        in_specs=[cur(PW_Q), cur(PW_KD), prev(PW_KD), cur(PW_VD), prev(PW_VD),
                  _resident(bias.shape, (layer,))],
        out_specs=cur(ATTN_WIDTH),
        out_shape=jax.ShapeDtypeStruct((batch, seq_len, ATTN_WIDTH), BF16),
        compiler_params=_params("parallel", "parallel"),
        name="swa",
    )(q3, kd3, kd3, vd3, vd3, bias)
    return out.reshape(batch * seq_len, ATTN_WIDTH)


def _split2(x):
    hi = x.astype(BF16)
    return hi, (x - hi.astype(F32)).astype(BF16)


def _wkv_body(r_ref, k_ref, v_ref, lw_ref, a_ref, g_ref,
              kk_ref, ka_ref, rk_ref, gng_ref, gnb_ref, o_ref, s_ref):
    t = pl.program_id(1)
    tblk, width = r_ref.shape
    n_pairs = width // PAIR

    @pl.when(t == 0)
    def _():
        s_ref[...] = jnp.zeros(s_ref.shape, F32)

    lane = lax.broadcasted_iota(jnp.int32, (1, PAIR), 1)
    lo = lane < RW_HEAD
    ri = lax.broadcasted_iota(jnp.int32, (PAIR, PAIR), 0)
    ci = lax.broadcasted_iota(jnp.int32, (PAIR, PAIR), 1)
    same_head = (ri >= RW_HEAD) == (ci >= RW_HEAD)
    head_ones = jnp.where(same_head, 1.0, 0.0).astype(BF16)
    eye = ri == ci
    tok_r = lax.broadcasted_iota(jnp.int32, (CHUNK, PAIR), 0)
    tok_c = lax.broadcasted_iota(jnp.int32, (CHUNK, PAIR), 1) & (CHUNK - 1)
    strict, incl, diag = tok_c < tok_r, tok_c <= tok_r, tok_c == tok_r
    rt_ = lax.broadcasted_iota(jnp.int32, (tblk, tblk), 0)
    ct_ = lax.broadcasted_iota(jnp.int32, (tblk, tblk), 1)
    same_chunk = (rt_ & -CHUNK) == (ct_ & -CHUNK)
    chunk_tril = jnp.where(jnp.logical_and(same_chunk, ct_ <= rt_), 1.0, 0.0).astype(BF16)

    def seg_sum(x):
        xb = x.astype(BF16)
        return jnp.concatenate(
            [_dot(xb[:, p * PAIR:(p + 1) * PAIR], head_ones) for p in range(n_pairs)], axis=1)

    def stack(x):
        x = x.astype(BF16)
        zero = jnp.zeros((), BF16)
        return jnp.concatenate([jnp.where(lo, x, zero), jnp.where(lo, zero, x)], axis=0)

    def per_head(x):
        return jnp.where(jnp.concatenate([same_head] * (x.shape[1] // PAIR), axis=1), x, 0.0)

    r, k, v, lw, a = r_ref[...], k_ref[...], v_ref[...], lw_ref[...], a_ref[...]
    kk = k * kk_ref[...]
    kk = kk / jnp.maximum(jnp.sqrt(seg_sum(kk * kk)), 1e-12)
    k = k * (1.0 + (a - 1.0) * ka_ref[...])
    b = kk * a
    bonus = seg_sum(r * k * rk_ref[...]) * v

    lw_hi, lw_lo = _split2(lw)
    cum = _dot(chunk_tril, lw_hi) + _dot(chunk_tril, lw_lo)

    n_chunks = tblk // CHUNK
    units = [(p, c) for c in range(n_chunks) for p in range(n_pairs)]
    idx = range(len(units))
    n_levels = int(math.log2(CHUNK))
    at, rt, lhs, rhs, bh, kh, vd, vs, dec = [], [], [], [], [], [], [], [], []
    for p, c in units:
        rows, cols = slice(c * CHUNK, (c + 1) * CHUNK), slice(p * PAIR, (p + 1) * PAIR)
        cu, lwu = cum[rows, cols], lw[rows, cols]
        ru, ku, bu, vu = r[rows, cols], k[rows, cols], b[rows, cols], v[rows, cols]
        end = cu[CHUNK - 1:CHUNK, :]
        e_pos, e_neg = jnp.exp(cu), jnp.exp(-cu)
        e_prev, e_end = jnp.exp(cu - lwu), jnp.exp(end - cu)
        at.append(-kk[rows, cols] * e_prev)
        rt.append(ru * e_pos)
        lhs.append(jnp.concatenate([at[-1], rt[-1]], axis=0).astype(BF16))
        rhs.append(jnp.concatenate([stack(bu * e_neg), stack(ku * e_neg)], axis=0))
        bh.append(jnp.transpose(bu * e_end).astype(BF16))
        kh.append(jnp.transpose(ku * e_end).astype(BF16))
        vd.append(vu.astype(BF16))
        vs.append(stack(vu))
        dec.append(jnp.where(eye, jnp.exp(end), 0.0))

    g = [_dot_nt(lhs[u], rhs[u]) for u in idx]
    a_ab = [jnp.where(strict, g[u][:CHUNK, :PAIR], 0.0) for u in idx]
    a_rb = [jnp.where(incl, g[u][CHUNK:, :PAIR], 0.0).astype(BF16) for u in idx]
    a_kv = [jnp.concatenate([jnp.where(strict, g[u][:CHUNK, PAIR:], 0.0),
                             jnp.where(incl, g[u][CHUNK:, PAIR:], 0.0)], axis=0).astype(BF16) for u in idx]
    akv = [_dot(a_kv[u], vs[u]) for u in idx]
    khv = [_dot(kh[u], vd[u]) for u in idx]

    inv = [jnp.where(diag, 1.0, a_ab[u]) for u in idx]
    pw = [a_ab[u].astype(BF16) for u in idx]
    pw = [_dot(pw[u], stack(pw[u])).astype(BF16) for u in idx]
    for level in range(1, n_levels):
        if level + 1 < n_levels:
            both = [_dot(pw[u], jnp.concatenate([stack(inv[u]), stack(pw[u])], axis=1)) for u in idx]
            inv = [inv[u] + both[u][:, :PAIR] for u in idx]
            pw = [both[u][:, PAIR:].astype(BF16) for u in idx]
        else:
            inv = [inv[u] + _dot(pw[u], stack(inv[u])) for u in idx]
    x = [_dot(inv[u].astype(BF16),
              jnp.concatenate([stack(at[u]), stack(akv[u][:CHUNK])], axis=1)).astype(BF16) for u in idx]

    qy = [_dot(a_rb[u], jnp.concatenate([stack(x[u][:, :PAIR]), stack(x[u][:, PAIR:])], axis=1))
          + jnp.concatenate([rt[u], akv[u][CHUNK:]], axis=1) for u in idx]
    mc = [per_head(_dot(bh[u], x[u])) + jnp.concatenate([dec[u], per_head(khv[u])], axis=1)
          for u in idx]
    q_hat = [qy[u][:, :PAIR].astype(BF16) for u in idx]
    m_hat = [mc[u][:, :PAIR].astype(BF16) for u in idx]

    state = [s_ref[p] for p in range(n_pairs)]
    ys = [[None] * n_chunks for _ in range(n_pairs)]
    for u, (p, c) in enumerate(units):
        sb = state[p].astype(BF16)
        ys[p][c] = _dot(q_hat[u], sb) + qy[u][:, PAIR:]
        state[p] = _dot(m_hat[u], sb) + mc[u][:, PAIR:]
    for p in range(n_pairs):
        s_ref[p] = state[p]

    y = jnp.concatenate([jnp.concatenate(ys[p], axis=0) for p in range(n_pairs)], axis=1)
    inv_n = 1.0 / RW_HEAD
    d = y - seg_sum(y) * inv_n
    var = seg_sum(d * d) * inv_n
    y = d * lax.rsqrt(var + GN_EPS) * gng_ref[...] + gnb_ref[...]
    o_ref[...] = ((y + bonus) * g_ref[...].astype(F32)).astype(BF16)


def _wkv7(r, k, v, lw, a, g, k_k, k_a, r_k, gn_g, gn_b, batch, seq_len):
    tok = pl.BlockSpec((None, WKV_TBLK, RW_WIDTH), lambda b, t: (b, t, 0))
    shape3 = (batch, seq_len, RW_WIDTH)
    params = (k_k, k_a, r_k, gn_g, gn_b)
    out = pl.pallas_call(
        _wkv_body,
        grid=(batch, seq_len // WKV_TBLK),
        in_specs=[tok] * 6 + [_resident(t.shape) for t in params],
        out_specs=tok,
        out_shape=jax.ShapeDtypeStruct(shape3, BF16),
        scratch_shapes=[pltpu.VMEM((N_PAIRS, PAIR, PAIR), F32)],
        compiler_params=_params("parallel", "arbitrary"),
        name="wkv7",
    )(*(t.reshape(shape3) for t in (r, k, v, lw, a, g)), *params)
    return out.reshape(batch * seq_len, RW_WIDTH)


def _mix_ln_body(attn_ref, rw_ref, gab_ref, x_ref, wba_ref, wbb_ref, wo_ref, g_ref, b_ref, o_ref):
    half = x_ref.shape[0] // 2
    for part in range(2):
        rows = slice(part * half, (part + 1) * half)
        gab = gab_ref[rows, :]
        merged = (gab[:, :D_MODEL].astype(F32) * _dot(attn_ref[rows, :], wba_ref[...])
                  + gab[:, D_MODEL:].astype(F32) * _dot(rw_ref[rows, :], wbb_ref[...]))
        mix = _dot(merged.astype(BF16), wo_ref[...])
        o_ref[rows, :] = _layer_norm(ALPHA * x_ref[rows, :] + mix, g_ref[...], b_ref[...])


def _mix_ln(attn, rw, gab, x, layer, wba, wbb, wo, g, b):
    m = x.shape[0]
    tile = lambda n: pl.BlockSpec((TM, n), lambda i: (i, 0))
    return pl.pallas_call(
        _mix_ln_body,
        grid=(m // TM,),
        in_specs=[tile(ATTN_WIDTH), tile(RW_WIDTH), tile(PW_GATES), tile(D_MODEL)]
        + [_resident(t.shape, (layer,)) for t in (wba, wbb, wo)] + [_resident(g.shape), _resident(b.shape)],
        out_specs=tile(D_MODEL),
        out_shape=jax.ShapeDtypeStruct((m, D_MODEL), F32),
        compiler_params=_params("parallel"),
        name="mix_ln",
    )(attn, rw, gab, x, wba, wbb, wo, g, b)


def _t5_bucket(n):
    max_exact = N_BUCKETS // 2
    nf = jnp.maximum(n, 1).astype(F32)
    large = max_exact + (jnp.log(nf / max_exact) / math.log(MAX_DISTANCE / max_exact)
                         * (N_BUCKETS - max_exact)).astype(jnp.int32)
    large = jnp.minimum(large, N_BUCKETS - 1)
    return jnp.where(n < max_exact, n, large)


def _attention_bias(rel_bias, sinks):
    dist_bias = rel_bias[_t5_bucket(jnp.arange(WINDOW, dtype=jnp.int32))].T
    period = 3 * BLOCK
    neg = lambda n: jnp.full((ATTN_Q_HEADS, n), -jnp.inf, F32)
    row0 = jnp.concatenate([neg(1), dist_bias[:, ::-1].astype(F32), neg(period - 1 - WINDOW)], axis=1)
    bias = jnp.tile(row0, (1, BLOCK))[:, :BLOCK * (period - 1)]
    bias = bias.reshape(ATTN_Q_HEADS, BLOCK, period - 1)[:, :, :2 * BLOCK]
    key = jnp.arange(2 * BLOCK)
    bias = jnp.where(key == 0, sinks.astype(F32)[:, :, None, None], bias[None])
    return bias.reshape(sinks.shape[0], ATTN_KV_HEADS, ATTN_GROUP * BLOCK, 2 * BLOCK)


def kernel(x, ffn_w_gu, ffn_w_down, ln_g, ln_b, w_in, b_in, rel_bias, attn_sinks, shift_mu,
           rw_w0, rw_w2, rw_a0, rw_a2, rw_g2, rw_k_k, rw_k_a, rw_r_k, rw_gn_g, rw_gn_b,
           rw_v0, rw_v1, rw_v2, w_branch_attn, w_branch_rwkv, w_out):
    batch, seq_len, _ = x.shape
    depth = w_in.shape[0]
    assert seq_len % TM == 0 and seq_len % TM_FFN == 0 and seq_len % TQ == 0 and seq_len % WKV_TBLK == 0
    row = lambda t: t.reshape(1, -1)
    bf = lambda t: t.astype(BF16)
    wgu, wd, w_proj = bf(ffn_w_gu), bf(ffn_w_down), bf(w_in)
    wba, wbb, wo = bf(w_branch_attn), bf(w_branch_rwkv), bf(w_out)
    bias = _attention_bias(rel_bias, attn_sinks)
    h = x.reshape(batch * seq_len, D_MODEL)
    v_first = None
    for l in range(depth):
        h = _ffn_ln(h, wgu, wd, (l, 0), row(ln_g[l, 0]), row(ln_b[l, 0]))
        vmix = None if l == 0 else (v_first, row(rw_v0[l - 1]), bf(rw_v1[l - 1]), bf(rw_v2[l - 1]))
        gab, q, kd, vd, r, k, v, lw, a, g = _proj_prep(
            h, seq_len, l, w_proj, row(b_in[l]), row(shift_mu[l]), row(rw_w0[l]), bf(rw_w2[l]),
            row(rw_a0[l]), bf(rw_a2[l]), bf(rw_g2[l]), vmix)
        if l == 0:
            v_first = v
        attn = _swa(q, kd, vd, bias, l, batch, seq_len)
        rw = _wkv7(r, k, v, lw, a, g, row(rw_k_k[l]), row(rw_k_a[l]), row(rw_r_k[l]),
                   row(rw_gn_g[l]), row(rw_gn_b[l]), batch, seq_len)
        h = _mix_ln(attn, rw, gab, h, l, wba, wbb, wo, row(ln_g[l, 1]), row(ln_b[l, 1]))
        h = _ffn_ln(h, wgu, wd, (l, 1), row(ln_g[l, 2]), row(ln_b[l, 2]))
    return h.reshape(batch, seq_len, D_MODEL)
```

```python
import functools
import math

import jax
import jax.numpy as jnp
from jax import lax
from jax.experimental import pallas as pl
from jax.experimental.pallas import tpu as pltpu

F32 = jnp.float32
BF16 = jnp.bfloat16

D_MODEL = 1024
DEPTH = 2
HEAD_DIM = 64
ATTN_Q_HEADS = 8
ATTN_KV_HEADS = 2
ATTN_GROUP = ATTN_Q_HEADS // ATTN_KV_HEADS
ATTN_WIDTH = ATTN_Q_HEADS * HEAD_DIM
KV_WIDTH = ATTN_KV_HEADS * HEAD_DIM
WINDOW = 128
BLOCK = 128
N_BUCKETS = 32
MAX_DISTANCE = 128
RW_HEADS = 8
RW_HEAD = 64
RW_WIDTH = RW_HEADS * RW_HEAD
DECAY_RANK = 64
A_RANK = 64
V_RANK = 32
G_RANK = 128
GN_EPS = 64e-5
D_FF = 2816
ALPHA = (2 * DEPTH) ** 0.25
LN_EPS = 1e-5
RW_COLS = 3 * RW_WIDTH + DECAY_RANK + A_RANK + G_RANK

LANES = 128
MXU_DIM = 256
VMEM_LIMIT = 56 * 1024 * 1024

TM = 512
TM_FFN = 1024
TM_MIX = 1024
FF_CHUNK = MXU_DIM
N_FF_CHUNKS = D_FF // FF_CHUNK
TQ = 512
CHUNK = 64
WKV_TBLK = 512
PAIR = 2 * RW_HEAD
N_PAIRS = RW_WIDTH // PAIR

PW_GATES = 2 * D_MODEL
PW_Q = ATTN_WIDTH
PW_KD = 2 * KV_WIDTH
PW_VD = 2 * KV_WIDTH
PO_Q = PW_GATES
PO_RW = PO_Q + ATTN_WIDTH + 2 * KV_WIDTH
PROJ_COLS = PO_RW + RW_COLS


def _dot(a, b):
    return jnp.dot(a, b, preferred_element_type=F32)


def _dot_nt(a, b):
    return lax.dot_general(a, b, (((1,), (1,)), ((), ())), preferred_element_type=F32)


def _layer_norm(y, g, b):
    mu = jnp.mean(y, axis=-1, keepdims=True)
    d = y - mu
    var = jnp.mean(d * d, axis=-1, keepdims=True)
    return d * lax.rsqrt(var + LN_EPS) * g + b


def _resident(shape, lead=()):
    rest = len(shape) - len(lead)
    block = (None,) * len(lead) + tuple(shape[len(lead):])
    return pl.BlockSpec(block, lambda *_: tuple(lead) + (0,) * rest, pipeline_mode=pl.Buffered(1))


def _params(*sem):
    return pltpu.CompilerParams(dimension_semantics=sem, vmem_limit_bytes=VMEM_LIMIT)


def _ffn_ln_body(x_ref, wgu_ref, wd_ref, g_ref, b_ref, o_ref):
    x = x_ref[...]
    xb = x.astype(BF16)
    acc = jnp.zeros(x.shape, F32)
    for j in range(N_FF_CHUNKS):
        cols = slice(j * FF_CHUNK, (j + 1) * FF_CHUNK)
        gate = _dot(xb, wgu_ref[:, cols])
        up = _dot(xb, wgu_ref[:, D_FF + j * FF_CHUNK:D_FF + (j + 1) * FF_CHUNK])
        h = jax.nn.silu(gate) * up
        acc = acc + _dot(h.astype(BF16), wd_ref[cols, :])
    o_ref[...] = _layer_norm(ALPHA * x + 0.5 * acc, g_ref[...], b_ref[...])


def _ffn_ln(x, wgu, wd, lead, g, b):
    m = x.shape[0]
    return pl.pallas_call(
        _ffn_ln_body,
        grid=(m // TM_FFN,),
        in_specs=[pl.BlockSpec((TM_FFN, D_MODEL), lambda i: (i, 0)),
                  _resident(wgu.shape, lead), _resident(wd.shape, lead),
                  _resident(g.shape), _resident(b.shape)],
        out_specs=pl.BlockSpec((TM_FFN, D_MODEL), lambda i: (i, 0)),
        out_shape=jax.ShapeDtypeStruct((m, D_MODEL), F32),
        compiler_params=_params("parallel"),
        name="ffn_ln",
    )(x, wgu, wd, g, b)


def _softplus(z):
    return jnp.maximum(z, 0.0) + jnp.log1p(jnp.exp(-jnp.abs(z)))


def _proj_body(tiles_per_seq, has_vmix, *refs):
    (x_ref, w_ref, b_ref, mu_ref, w0_ref, w2_ref, a0_ref, a2_ref, g2_ref) = refs[:9]
    refs = refs[9:]
    if has_vmix:
        vf_ref, v0_ref, v1_ref, v2_ref = refs[:4]
        refs = refs[4:]
    (gab_ref, q_ref, kd_ref, vd_ref, r_ref, k_ref, v_ref, lw_ref, a_ref, g_ref, carry_ref) = refs

    i = pl.program_id(0)
    tm = x_ref.shape[0]

    @pl.when(i % tiles_per_seq == 0)
    def _():
        carry_ref[...] = jnp.zeros(carry_ref.shape, F32)

    xb = x_ref[...].astype(BF16)
    part = lambda lo, hi: _dot(xb, w_ref[:, lo:hi]) + b_ref[:, lo:hi]
    u = part(PO_RW, PROJ_COLS)
    gates = part(0, PW_GATES)

    before_tile = carry_ref[7:8, :]
    carry_ref[...] = u[tm - 8:, :]
    row = lax.broadcasted_iota(jnp.int32, (tm, 1), 0)
    u_prev = jnp.where(row == 0, before_tile, pltpu.roll(u, 1, 0))
    u = u + (u_prev - u) * mu_ref[...]

    c = RW_WIDTH
    r = u[:, :c]
    k = u[:, c:2 * c]
    v = u[:, 2 * c:3 * c]
    o = 3 * c
    xw = u[:, o:o + DECAY_RANK]
    xa = u[:, o + DECAY_RANK:o + DECAY_RANK + A_RANK]
    xg = u[:, o + DECAY_RANK + A_RANK:]

    w = -_softplus(-(w0_ref[...] + _dot(jnp.tanh(xw).astype(BF16), w2_ref[...]))) - 0.5
    lw_ref[...] = -jnp.exp(w)
    a_ref[...] = jax.nn.sigmoid(a0_ref[...] + _dot(xa.astype(BF16), a2_ref[...]))
    g_ref[...] = _dot(jax.nn.sigmoid(xg).astype(BF16), g2_ref[...]).astype(BF16)
    if has_vmix:
        low = _dot(v.astype(BF16), v1_ref[...])
        mix = jax.nn.sigmoid(v0_ref[...] + _dot(low.astype(BF16), v2_ref[...]))
        v = v + (vf_ref[...] - v) * mix
    r_ref[...] = r
    k_ref[...] = k
    v_ref[...] = v

    qkv = part(PO_Q, PO_RW)
    gab_ref[...] = jax.nn.sigmoid(gates).astype(BF16)
    q_ref[...] = (qkv[:, :ATTN_WIDTH] * (HEAD_DIM ** -0.5)).astype(BF16)
    lo = lax.broadcasted_iota(jnp.int32, (1, LANES), 1) < HEAD_DIM
    for ref, at in ((kd_ref, ATTN_WIDTH), (vd_ref, ATTN_WIDTH + KV_WIDTH)):
        pair = qkv[:, at:at + KV_WIDTH]
        swapped = pltpu.roll(pair, HEAD_DIM, 1)
        ref[...] = jnp.concatenate([jnp.where(lo, pair, swapped),
                                    jnp.where(lo, swapped, pair)], axis=1).astype(BF16)


def _proj_prep(x, seq_len, layer, w, b, mu, w0, w2, a0, a2, g2, vmix):
    m = x.shape[0]
    tile = lambda n: pl.BlockSpec((TM, n), lambda i: (i, 0))
    has_vmix = vmix is not None
    args = [x, w, b, mu, w0, w2, a0, a2, g2]
    in_specs = [tile(D_MODEL), _resident(w.shape, (layer,))] + [_resident(t.shape) for t in args[2:]]
    if has_vmix:
        vf, v0, v1, v2 = vmix
        args += [vf, v0, v1, v2]
        in_specs += [tile(RW_WIDTH), _resident(v0.shape), _resident(v1.shape), _resident(v2.shape)]
    widths = [(PW_GATES, BF16), (PW_Q, BF16), (PW_KD, BF16), (PW_VD, BF16)] \
        + [(RW_WIDTH, F32)] * 5 + [(RW_WIDTH, BF16)]
    return pl.pallas_call(
        functools.partial(_proj_body, seq_len // TM, has_vmix),
        grid=(m // TM,),
        in_specs=in_specs,
        out_specs=[tile(n) for n, _ in widths],
        out_shape=[jax.ShapeDtypeStruct((m, n), dt) for n, dt in widths],
        scratch_shapes=[pltpu.VMEM((8, RW_COLS), F32)],
        compiler_params=_params("arbitrary"),
        name="proj_prep",
    )(*args)


def _swa_body(q_ref, kc_ref, kp_ref, vc_ref, vp_ref, bias_ref, o_ref):
    i = pl.program_id(1)
    lane = lax.broadcasted_iota(jnp.int32, (1, LANES), 1)
    lo = lane < HEAD_DIM
    key = lax.broadcasted_iota(jnp.int32, (1, 2 * BLOCK), 1)
    not_key0 = lax.broadcasted_iota(jnp.int32, (2 * BLOCK, 1), 0) > 0
    no_prev = jnp.logical_and(i == 0, jnp.logical_and(key < BLOCK, key > 0))
    ones = jnp.ones((2 * BLOCK, LANES), BF16)
    zero = jnp.zeros((), BF16)
    units = [(j, h) for j in range(TQ // BLOCK) for h in range(ATTN_KV_HEADS)]
    idx = range(len(units))
    scores, v_lo, v_hi = [], [], []
    for j, h in units:
        rows = slice(j * BLOCK, (j + 1) * BLOCK)
        cols = slice(h * LANES, (h + 1) * LANES)
        if j == 0:
            k_prev, v_prev = kp_ref[:, cols], vp_ref[:, cols]
        else:
            before = slice((j - 1) * BLOCK, j * BLOCK)
            k_prev, v_prev = kc_ref[before, cols], vc_ref[before, cols]
        kb = jnp.concatenate([k_prev, kc_ref[rows, cols]], axis=0)
        vb = jnp.concatenate([v_prev, vc_ref[rows, cols]], axis=0)
        kb, vb = jnp.where(not_key0, kb, zero), jnp.where(not_key0, vb, zero)
        parts = []
        for p in (2 * h, 2 * h + 1):
            qp = q_ref[rows, p * LANES:(p + 1) * LANES]
            parts += [jnp.where(lo, qp, zero), jnp.where(lo, zero, qp)]
        s = _dot_nt(jnp.concatenate(parts, axis=0), kb) + bias_ref[h]
        if j == 0:
            s = jnp.where(no_prev, -jnp.inf, s)
        scores.append(s)
        v_lo.append(jnp.where(lo, vb, zero))
        v_hi.append(jnp.where(lo, zero, vb))
    mx = [jnp.max(scores[u], axis=-1, keepdims=True) for u in idx]
    eb = [jnp.exp(scores[u] - mx[u]).astype(BF16) for u in idx]
    den = [_dot(eb[u], ones) for u in idx]
    for u, (j, h) in enumerate(units):
        rows = slice(j * BLOCK, (j + 1) * BLOCK)
        for n, p in enumerate((2 * h, 2 * h + 1)):
            top = slice(2 * n * BLOCK, (2 * n + 1) * BLOCK)
            bot = slice((2 * n + 1) * BLOCK, (2 * n + 2) * BLOCK)
            out = (_dot(eb[u][top], v_lo[u]) / den[u][top]
                   + _dot(eb[u][bot], v_hi[u]) / den[u][bot])
            o_ref[rows, p * LANES:(p + 1) * LANES] = out.astype(BF16)


def _swa(q, kd, vd, bias, layer, batch, seq_len):
    q3 = q.reshape(batch, seq_len, PW_Q)
    kd3 = kd.reshape(batch, seq_len, PW_KD)
    vd3 = vd.reshape(batch, seq_len, PW_VD)
    cur = lambda n: pl.BlockSpec((None, TQ, n), lambda b, i: (b, i, 0))
    prev = lambda n: pl.BlockSpec(
        (None, BLOCK, n), lambda b, i: (b, jnp.maximum(i * (TQ // BLOCK) - 1, 0), 0))
    out = pl.pallas_call(
        _swa_body,
        grid=(batch, seq_len // TQ),
        in_specs=[cur(PW_Q), cur(PW_KD), prev(PW_KD), cur(PW_VD), prev(PW_VD),
                  _resident(bias.shape, (layer,))],
        out_specs=cur(ATTN_WIDTH),
        out_shape=jax.ShapeDtypeStruct((batch, seq_len, ATTN_WIDTH), BF16),
        compiler_params=_params("parallel", "parallel"),
        name="swa",
    )(q3, kd3, kd3, vd3, vd3, bias)
    return out.reshape(batch * seq_len, ATTN_WIDTH)


def _split2(x):
    hi = x.astype(BF16)
    return hi, (x - hi.astype(F32)).astype(BF16)


def _wkv_body(r_ref, k_ref, v_ref, lw_ref, a_ref, g_ref,
              kk_ref, ka_ref, rk_ref, gng_ref, gnb_ref, o_ref, s_ref):
    t = pl.program_id(1)
    tblk, width = r_ref.shape
    n_pairs = width // PAIR

    @pl.when(t == 0)
    def _():
        s_ref[...] = jnp.zeros(s_ref.shape, F32)

    lane = lax.broadcasted_iota(jnp.int32, (1, PAIR), 1)
    lo = lane < RW_HEAD
    ri = lax.broadcasted_iota(jnp.int32, (PAIR, PAIR), 0)
    ci = lax.broadcasted_iota(jnp.int32, (PAIR, PAIR), 1)
    same_head = (ri >= RW_HEAD) == (ci >= RW_HEAD)
    head_ones = jnp.where(same_head, 1.0, 0.0).astype(BF16)
    eye = ri == ci
    tok_r = lax.broadcasted_iota(jnp.int32, (CHUNK, PAIR), 0)
    tok_c = lax.broadcasted_iota(jnp.int32, (CHUNK, PAIR), 1) & (CHUNK - 1)
    strict, incl, diag = tok_c < tok_r, tok_c <= tok_r, tok_c == tok_r
    rt_ = lax.broadcasted_iota(jnp.int32, (MXU_DIM, MXU_DIM), 0)
    ct_ = lax.broadcasted_iota(jnp.int32, (MXU_DIM, MXU_DIM), 1)
    same_chunk = (rt_ & -CHUNK) == (ct_ & -CHUNK)
    chunk_tril = jnp.where(jnp.logical_and(same_chunk, ct_ <= rt_), 1.0, 0.0).astype(BF16)

    def seg_sum(x):
        xb = x.astype(BF16)
        return jnp.concatenate(
            [_dot(xb[:, p * PAIR:(p + 1) * PAIR], head_ones) for p in range(n_pairs)], axis=1)

    def stack(x):
        x = x.astype(BF16)
        zero = jnp.zeros((), BF16)
        return jnp.concatenate([jnp.where(lo, x, zero), jnp.where(lo, zero, x)], axis=0)

    def per_head(x):
        return jnp.where(jnp.concatenate([same_head] * (x.shape[1] // PAIR), axis=1), x, 0.0)

    r, k, v, lw, a = r_ref[...], k_ref[...], v_ref[...], lw_ref[...], a_ref[...]
    kk = k * kk_ref[...]
    kk = kk / jnp.maximum(jnp.sqrt(seg_sum(kk * kk)), 1e-12)
    k = k * (1.0 + (a - 1.0) * ka_ref[...])
    b = kk * a
    bonus = seg_sum(r * k * rk_ref[...]) * v

    lw_hi, lw_lo = _split2(lw)
    cum = jnp.concatenate(
        [_dot(chunk_tril, lw_hi[i:i + MXU_DIM]) + _dot(chunk_tril, lw_lo[i:i + MXU_DIM])
         for i in range(0, tblk, MXU_DIM)], axis=0)

    n_chunks = tblk // CHUNK
    units = [(p, c) for c in range(n_chunks) for p in range(n_pairs)]
    idx = range(len(units))
    n_levels = int(math.log2(CHUNK))
    at, rt, lhs, rhs, bh, kh, vd, vs, dec = [], [], [], [], [], [], [], [], []
    for p, c in units:
        rows, cols = slice(c * CHUNK, (c + 1) * CHUNK), slice(p * PAIR, (p + 1) * PAIR)
        cu, lwu = cum[rows, cols], lw[rows, cols]
        ru, ku, bu, vu = r[rows, cols], k[rows, cols], b[rows, cols], v[rows, cols]
        end = cu[CHUNK - 1:CHUNK, :]
        e_pos, e_neg = jnp.exp(cu), jnp.exp(-cu)
        e_prev, e_end = jnp.exp(cu - lwu), jnp.exp(end - cu)
        at.append(-kk[rows, cols] * e_prev)
        rt.append(ru * e_pos)
        lhs.append(jnp.concatenate([at[-1], rt[-1]], axis=0).astype(BF16))
        rhs.append(jnp.concatenate([stack(bu * e_neg), stack(ku * e_neg)], axis=0))
        bh.append(jnp.transpose(bu * e_end).astype(BF16))
        kh.append(jnp.transpose(ku * e_end).astype(BF16))
        vd.append(vu.astype(BF16))
        vs.append(stack(vu))
        dec.append(jnp.where(eye, jnp.exp(end), 0.0))

    g = [_dot_nt(lhs[u], rhs[u]) for u in idx]
    a_ab = [jnp.where(strict, g[u][:CHUNK, :PAIR], 0.0) for u in idx]
    a_rb = [jnp.where(incl, g[u][CHUNK:, :PAIR], 0.0).astype(BF16) for u in idx]
    a_kv = [jnp.concatenate([jnp.where(strict, g[u][:CHUNK, PAIR:], 0.0),
                             jnp.where(incl, g[u][CHUNK:, PAIR:], 0.0)], axis=0).astype(BF16) for u in idx]
    akv = [_dot(a_kv[u], vs[u]) for u in idx]
    khv = [_dot(kh[u], vd[u]) for u in idx]

    inv = [jnp.where(diag, 1.0, a_ab[u]) for u in idx]
    pw = [a_ab[u].astype(BF16) for u in idx]
    pw = [_dot(pw[u], stack(pw[u])).astype(BF16) for u in idx]
    for level in range(1, n_levels):
        if level + 1 < n_levels:
            both = [_dot(pw[u], jnp.concatenate([stack(inv[u]), stack(pw[u])], axis=1)) for u in idx]
            inv = [inv[u] + both[u][:, :PAIR] for u in idx]
            pw = [both[u][:, PAIR:].astype(BF16) for u in idx]
        else:
            inv = [inv[u] + _dot(pw[u], stack(inv[u])) for u in idx]
    x = [_dot(inv[u].astype(BF16),
              jnp.concatenate([stack(at[u]), stack(akv[u][:CHUNK])], axis=1)).astype(BF16) for u in idx]

    qy = [_dot(a_rb[u], jnp.concatenate([stack(x[u][:, :PAIR]), stack(x[u][:, PAIR:])], axis=1))
          + jnp.concatenate([rt[u], akv[u][CHUNK:]], axis=1) for u in idx]
    mc = [per_head(_dot(bh[u], x[u])) + jnp.concatenate([dec[u], per_head(khv[u])], axis=1)
          for u in idx]
    q_hat = [qy[u][:, :PAIR].astype(BF16) for u in idx]
    m_hat = [mc[u][:, :PAIR].astype(BF16) for u in idx]

    state = [s_ref[p] for p in range(n_pairs)]
    ys = [[None] * n_chunks for _ in range(n_pairs)]
    for u, (p, c) in enumerate(units):
        sb = state[p].astype(BF16)
        ys[p][c] = _dot(q_hat[u], sb) + qy[u][:, PAIR:]
        state[p] = _dot(m_hat[u], sb) + mc[u][:, PAIR:]
    for p in range(n_pairs):
        s_ref[p] = state[p]

    y = jnp.concatenate([jnp.concatenate(ys[p], axis=0) for p in range(n_pairs)], axis=1)
    inv_n = 1.0 / RW_HEAD
    d = y - seg_sum(y) * inv_n
    var = seg_sum(d * d) * inv_n
    y = d * lax.rsqrt(var + GN_EPS) * gng_ref[...] + gnb_ref[...]
    o_ref[...] = ((y + bonus) * g_ref[...].astype(F32)).astype(BF16)


def _wkv7(r, k, v, lw, a, g, k_k, k_a, r_k, gn_g, gn_b, batch, seq_len):
    tok = pl.BlockSpec((None, WKV_TBLK, RW_WIDTH), lambda b, t: (b, t, 0))
    shape3 = (batch, seq_len, RW_WIDTH)
    params = (k_k, k_a, r_k, gn_g, gn_b)
    out = pl.pallas_call(
        _wkv_body,
        grid=(batch, seq_len // WKV_TBLK),
        in_specs=[tok] * 6 + [_resident(t.shape) for t in params],
        out_specs=tok,
        out_shape=jax.ShapeDtypeStruct(shape3, BF16),
        scratch_shapes=[pltpu.VMEM((N_PAIRS, PAIR, PAIR), F32)],
        compiler_params=_params("parallel", "arbitrary"),
        name="wkv7",
    )(*(t.reshape(shape3) for t in (r, k, v, lw, a, g)), *params)
    return out.reshape(batch * seq_len, RW_WIDTH)


def _mix_ln_body(attn_ref, rw_ref, gab_ref, x_ref, wba_ref, wbb_ref, wo_ref, g_ref, b_ref, o_ref):
    half = x_ref.shape[0] // 2
    for part in range(2):
        rows = slice(part * half, (part + 1) * half)
        gab = gab_ref[rows, :]
        merged = (gab[:, :D_MODEL].astype(F32) * _dot(attn_ref[rows, :], wba_ref[...])
                  + gab[:, D_MODEL:].astype(F32) * _dot(rw_ref[rows, :], wbb_ref[...]))
        mix = _dot(merged.astype(BF16), wo_ref[...])
        o_ref[rows, :] = _layer_norm(ALPHA * x_ref[rows, :] + mix, g_ref[...], b_ref[...])


def _mix_ln(attn, rw, gab, x, layer, wba, wbb, wo, g, b):
    m = x.shape[0]
    tile = lambda n: pl.BlockSpec((TM_MIX, n), lambda i: (i, 0))
    return pl.pallas_call(
        _mix_ln_body,
        grid=(m // TM_MIX,),
        in_specs=[tile(ATTN_WIDTH), tile(RW_WIDTH), tile(PW_GATES), tile(D_MODEL)]
        + [_resident(t.shape, (layer,)) for t in (wba, wbb, wo)] + [_resident(g.shape), _resident(b.shape)],
        out_specs=tile(D_MODEL),
        out_shape=jax.ShapeDtypeStruct((m, D_MODEL), F32),
        compiler_params=_params("parallel"),
        name="mix_ln",
    )(attn, rw, gab, x, wba, wbb, wo, g, b)


def _t5_bucket(n):
    max_exact = N_BUCKETS // 2
    nf = jnp.maximum(n, 1).astype(F32)
    large = max_exact + (jnp.log(nf / max_exact) / math.log(MAX_DISTANCE / max_exact)
                         * (N_BUCKETS - max_exact)).astype(jnp.int32)
    large = jnp.minimum(large, N_BUCKETS - 1)
    return jnp.where(n < max_exact, n, large)


def _attention_bias(rel_bias, sinks):
    dist_bias = rel_bias[_t5_bucket(jnp.arange(WINDOW, dtype=jnp.int32))].T
    period = 3 * BLOCK
    neg = lambda n: jnp.full((ATTN_Q_HEADS, n), -jnp.inf, F32)
    row0 = jnp.concatenate([neg(1), dist_bias[:, ::-1].astype(F32), neg(period - 1 - WINDOW)], axis=1)
    bias = jnp.tile(row0, (1, BLOCK))[:, :BLOCK * (period - 1)]
    bias = bias.reshape(ATTN_Q_HEADS, BLOCK, period - 1)[:, :, :2 * BLOCK]
    key = jnp.arange(2 * BLOCK)
    bias = jnp.where(key == 0, sinks.astype(F32)[:, :, None, None], bias[None])
    return bias.reshape(sinks.shape[0], ATTN_KV_HEADS, ATTN_GROUP * BLOCK, 2 * BLOCK)


def kernel(x, ffn_w_gu, ffn_w_down, ln_g, ln_b, w_in, b_in, rel_bias, attn_sinks, shift_mu,
           rw_w0, rw_w2, rw_a0, rw_a2, rw_g2, rw_k_k, rw_k_a, rw_r_k, rw_gn_g, rw_gn_b,
           rw_v0, rw_v1, rw_v2, w_branch_attn, w_branch_rwkv, w_out):
    batch, seq_len, _ = x.shape
    depth = w_in.shape[0]
    assert all(seq_len % n == 0 for n in (TM, TM_FFN, TM_MIX, TQ, WKV_TBLK))
    assert WKV_TBLK % MXU_DIM == 0
    row = lambda t: t.reshape(1, -1)
    bf = lambda t: t.astype(BF16)
    wgu, wd, w_proj = bf(ffn_w_gu), bf(ffn_w_down), bf(w_in)
    wba, wbb, wo = bf(w_branch_attn), bf(w_branch_rwkv), bf(w_out)
    bias = _attention_bias(rel_bias, attn_sinks)
    h = x.reshape(batch * seq_len, D_MODEL)
    v_first = None
    for l in range(depth):
        h = _ffn_ln(h, wgu, wd, (l, 0), row(ln_g[l, 0]), row(ln_b[l, 0]))
        vmix = None if l == 0 else (v_first, row(rw_v0[l - 1]), bf(rw_v1[l - 1]), bf(rw_v2[l - 1]))
        gab, q, kd, vd, r, k, v, lw, a, g = _proj_prep(
            h, seq_len, l, w_proj, row(b_in[l]), row(shift_mu[l]), row(rw_w0[l]), bf(rw_w2[l]),
            row(rw_a0[l]), bf(rw_a2[l]), bf(rw_g2[l]), vmix)
        if l == 0:
            v_first = v
        attn = _swa(q, kd, vd, bias, l, batch, seq_len)
        rw = _wkv7(r, k, v, lw, a, g, row(rw_k_k[l]), row(rw_k_a[l]), row(rw_r_k[l]),
                   row(rw_gn_g[l]), row(rw_gn_b[l]), batch, seq_len)
        h = _mix_ln(attn, rw, gab, h, l, wba, wbb, wo, row(ln_g[l, 1]), row(ln_b[l, 1]))
        h = _ffn_ln(h, wgu, wd, (l, 1), row(ln_g[l, 2]), row(ln_b[l, 2]))
    return h.reshape(batch, seq_len, D_MODEL)
```

```python
import functools
import math

import jax
import jax.numpy as jnp
from jax import lax
from jax.experimental import pallas as pl
from jax.experimental.pallas import tpu as pltpu

F32 = jnp.float32
BF16 = jnp.bfloat16

D_MODEL = 1024
DEPTH = 2
HEAD_DIM = 64
ATTN_Q_HEADS = 8
ATTN_KV_HEADS = 2
ATTN_GROUP = ATTN_Q_HEADS // ATTN_KV_HEADS
ATTN_WIDTH = ATTN_Q_HEADS * HEAD_DIM
KV_WIDTH = ATTN_KV_HEADS * HEAD_DIM
WINDOW = 128
BLOCK = 128
N_BUCKETS = 32
MAX_DISTANCE = 128
RW_HEADS = 8
RW_HEAD = 64
RW_WIDTH = RW_HEADS * RW_HEAD
DECAY_RANK = 64
A_RANK = 64
V_RANK = 32
G_RANK = 128
GN_EPS = 64e-5
D_FF = 2816
ALPHA = (2 * DEPTH) ** 0.25
LN_EPS = 1e-5
RW_COLS = 3 * RW_WIDTH + DECAY_RANK + A_RANK + G_RANK

LANES = 128
MXU_DIM = 256
VMEM_LIMIT = 56 * 1024 * 1024

TM = 512
TM_FFN = 1024
TM_MIX = 1024
FF_CHUNK = MXU_DIM
N_FF_CHUNKS = D_FF // FF_CHUNK
TQ = 1024
CHUNK = 64
WKV_TBLK = 512
PAIR = 2 * RW_HEAD
N_PAIRS = RW_WIDTH // PAIR

PW_GATES = 2 * D_MODEL
PW_Q = ATTN_WIDTH
PW_KD = 2 * KV_WIDTH
PW_VD = 2 * KV_WIDTH
PO_Q = PW_GATES
PO_RW = PO_Q + ATTN_WIDTH + 2 * KV_WIDTH
PROJ_COLS = PO_RW + RW_COLS


def _dot(a, b):
    return jnp.dot(a, b, preferred_element_type=F32)


def _dot_nt(a, b):
    return lax.dot_general(a, b, (((1,), (1,)), ((), ())), preferred_element_type=F32)


def _layer_norm(y, g, b):
    mu = jnp.mean(y, axis=-1, keepdims=True)
    d = y - mu
    var = jnp.mean(d * d, axis=-1, keepdims=True)
    return d * lax.rsqrt(var + LN_EPS) * g + b


def _resident(shape, lead=()):
    rest = len(shape) - len(lead)
    block = (None,) * len(lead) + tuple(shape[len(lead):])
    return pl.BlockSpec(block, lambda *_: tuple(lead) + (0,) * rest, pipeline_mode=pl.Buffered(1))


def _params(*sem):
    return pltpu.CompilerParams(dimension_semantics=sem, vmem_limit_bytes=VMEM_LIMIT)


def _ffn_ln_body(x_ref, wgu_ref, wd_ref, g_ref, b_ref, o_ref):
    x = x_ref[...]
    xb = x.astype(BF16)
    acc = jnp.zeros(x.shape, F32)
    for j in range(N_FF_CHUNKS):
        cols = slice(j * FF_CHUNK, (j + 1) * FF_CHUNK)
        gate = _dot(xb, wgu_ref[:, cols])
        up = _dot(xb, wgu_ref[:, D_FF + j * FF_CHUNK:D_FF + (j + 1) * FF_CHUNK])
        h = jax.nn.silu(gate) * up
        acc = acc + _dot(h.astype(BF16), wd_ref[cols, :])
    o_ref[...] = _layer_norm(ALPHA * x + 0.5 * acc, g_ref[...], b_ref[...])


def _ffn_ln(x, wgu, wd, lead, g, b):
    m = x.shape[0]
    return pl.pallas_call(
        _ffn_ln_body,
        grid=(m // TM_FFN,),
        in_specs=[pl.BlockSpec((TM_FFN, D_MODEL), lambda i: (i, 0)),
                  _resident(wgu.shape, lead), _resident(wd.shape, lead),
                  _resident(g.shape), _resident(b.shape)],
        out_specs=pl.BlockSpec((TM_FFN, D_MODEL), lambda i: (i, 0)),
        out_shape=jax.ShapeDtypeStruct((m, D_MODEL), F32),
        compiler_params=_params("parallel"),
        name="ffn_ln",
    )(x, wgu, wd, g, b)


def _softplus(z):
    return jnp.maximum(z, 0.0) + jnp.log1p(jnp.exp(-jnp.abs(z)))


def _proj_body(tiles_per_seq, has_vmix, *refs):
    (x_ref, w_ref, b_ref, mu_ref, w0_ref, w2_ref, a0_ref, a2_ref, g2_ref) = refs[:9]
    refs = refs[9:]
    if has_vmix:
        vf_ref, v0_ref, v1_ref, v2_ref = refs[:4]
        refs = refs[4:]
    (gab_ref, q_ref, kd_ref, vd_ref, r_ref, k_ref, v_ref, lw_ref, a_ref, g_ref, carry_ref) = refs

    i = pl.program_id(0)
    tm = x_ref.shape[0]

    @pl.when(i % tiles_per_seq == 0)
    def _():
        carry_ref[...] = jnp.zeros(carry_ref.shape, F32)

    xb = x_ref[...].astype(BF16)
    part = lambda lo, hi: _dot(xb, w_ref[:, lo:hi]) + b_ref[:, lo:hi]
    u = part(PO_RW, PROJ_COLS)
    gates = part(0, PW_GATES)

    before_tile = carry_ref[7:8, :]
    carry_ref[...] = u[tm - 8:, :]
    row = lax.broadcasted_iota(jnp.int32, (tm, 1), 0)
    u_prev = jnp.where(row == 0, before_tile, pltpu.roll(u, 1, 0))
    u = u + (u_prev - u) * mu_ref[...]

    c = RW_WIDTH
    r = u[:, :c]
    k = u[:, c:2 * c]
    v = u[:, 2 * c:3 * c]
    o = 3 * c
    xw = u[:, o:o + DECAY_RANK]
    xa = u[:, o + DECAY_RANK:o + DECAY_RANK + A_RANK]
    xg = u[:, o + DECAY_RANK + A_RANK:]

    w = -_softplus(-(w0_ref[...] + _dot(jnp.tanh(xw).astype(BF16), w2_ref[...]))) - 0.5
    lw_ref[...] = -jnp.exp(w)
    a_ref[...] = jax.nn.sigmoid(a0_ref[...] + _dot(xa.astype(BF16), a2_ref[...]))
    g_ref[...] = _dot(jax.nn.sigmoid(xg).astype(BF16), g2_ref[...]).astype(BF16)
    if has_vmix:
        low = _dot(v.astype(BF16), v1_ref[...])
        mix = jax.nn.sigmoid(v0_ref[...] + _dot(low.astype(BF16), v2_ref[...]))
        v = v + (vf_ref[...] - v) * mix
    r_ref[...] = r
    k_ref[...] = k
    v_ref[...] = v

    qkv = part(PO_Q, PO_RW)
    gab_ref[...] = jax.nn.sigmoid(gates).astype(BF16)
    q_ref[...] = (qkv[:, :ATTN_WIDTH] * (HEAD_DIM ** -0.5)).astype(BF16)
    lo = lax.broadcasted_iota(jnp.int32, (1, LANES), 1) < HEAD_DIM
    for ref, at in ((kd_ref, ATTN_WIDTH), (vd_ref, ATTN_WIDTH + KV_WIDTH)):
        pair = qkv[:, at:at + KV_WIDTH]
        swapped = pltpu.roll(pair, HEAD_DIM, 1)
        ref[...] = jnp.concatenate([jnp.where(lo, pair, swapped),
                                    jnp.where(lo, swapped, pair)], axis=1).astype(BF16)


def _proj_prep(x, seq_len, layer, w, b, mu, w0, w2, a0, a2, g2, vmix):
    m = x.shape[0]
    tile = lambda n: pl.BlockSpec((TM, n), lambda i: (i, 0))
    has_vmix = vmix is not None
    args = [x, w, b, mu, w0, w2, a0, a2, g2]
    in_specs = [tile(D_MODEL), _resident(w.shape, (layer,))] + [_resident(t.shape) for t in args[2:]]
    if has_vmix:
        vf, v0, v1, v2 = vmix
        args += [vf, v0, v1, v2]
        in_specs += [tile(RW_WIDTH), _resident(v0.shape), _resident(v1.shape), _resident(v2.shape)]
    widths = [(PW_GATES, BF16), (PW_Q, BF16), (PW_KD, BF16), (PW_VD, BF16)] \
        + [(RW_WIDTH, F32)] * 5 + [(RW_WIDTH, BF16)]
    return pl.pallas_call(
        functools.partial(_proj_body, seq_len // TM, has_vmix),
        grid=(m // TM,),
        in_specs=in_specs,
        out_specs=[tile(n) for n, _ in widths],
        out_shape=[jax.ShapeDtypeStruct((m, n), dt) for n, dt in widths],
        scratch_shapes=[pltpu.VMEM((8, RW_COLS), F32)],
        compiler_params=_params("arbitrary"),
        name="proj_prep",
    )(*args)


def _swa_body(q_ref, kc_ref, kp_ref, vc_ref, vp_ref, bias_ref, o_ref):
    i = pl.program_id(1)
    lane = lax.broadcasted_iota(jnp.int32, (1, LANES), 1)
    lo = lane < HEAD_DIM
    key = lax.broadcasted_iota(jnp.int32, (1, 2 * BLOCK), 1)
    not_key0 = lax.broadcasted_iota(jnp.int32, (2 * BLOCK, 1), 0) > 0
    no_prev = jnp.logical_and(i == 0, jnp.logical_and(key < BLOCK, key > 0))
    ones = jnp.ones((2 * BLOCK, LANES), BF16)
    zero = jnp.zeros((), BF16)
    units = [(j, h) for j in range(TQ // BLOCK) for h in range(ATTN_KV_HEADS)]
    idx = range(len(units))
    scores, v_lo, v_hi = [], [], []
    for j, h in units:
        rows = slice(j * BLOCK, (j + 1) * BLOCK)
        cols = slice(h * LANES, (h + 1) * LANES)
        if j == 0:
            k_prev, v_prev = kp_ref[:, cols], vp_ref[:, cols]
        else:
            before = slice((j - 1) * BLOCK, j * BLOCK)
            k_prev, v_prev = kc_ref[before, cols], vc_ref[before, cols]
        kb = jnp.concatenate([k_prev, kc_ref[rows, cols]], axis=0)
        vb = jnp.concatenate([v_prev, vc_ref[rows, cols]], axis=0)
        kb, vb = jnp.where(not_key0, kb, zero), jnp.where(not_key0, vb, zero)
        parts = []
        for p in (2 * h, 2 * h + 1):
            qp = q_ref[rows, p * LANES:(p + 1) * LANES]
            parts += [jnp.where(lo, qp, zero), jnp.where(lo, zero, qp)]
        s = _dot_nt(jnp.concatenate(parts, axis=0), kb) + bias_ref[h]
        if j == 0:
            s = jnp.where(no_prev, -jnp.inf, s)
        scores.append(s)
        v_lo.append(jnp.where(lo, vb, zero))
        v_hi.append(jnp.where(lo, zero, vb))
    mx = [jnp.max(scores[u], axis=-1, keepdims=True) for u in idx]
    eb = [jnp.exp(scores[u] - mx[u]).astype(BF16) for u in idx]
    den = [_dot(eb[u], ones) for u in idx]
    for u, (j, h) in enumerate(units):
        rows = slice(j * BLOCK, (j + 1) * BLOCK)
        for n, p in enumerate((2 * h, 2 * h + 1)):
            top = slice(2 * n * BLOCK, (2 * n + 1) * BLOCK)
            bot = slice((2 * n + 1) * BLOCK, (2 * n + 2) * BLOCK)
            out = (_dot(eb[u][top], v_lo[u]) / den[u][top]
                   + _dot(eb[u][bot], v_hi[u]) / den[u][bot])
            o_ref[rows, p * LANES:(p + 1) * LANES] = out.astype(BF16)


def _swa(q, kd, vd, bias, layer, batch, seq_len):
    q3 = q.reshape(batch, seq_len, PW_Q)
    kd3 = kd.reshape(batch, seq_len, PW_KD)
    vd3 = vd.reshape(batch, seq_len, PW_VD)
    cur = lambda n: pl.BlockSpec((None, TQ, n), lambda b, i: (b, i, 0))
    prev = lambda n: pl.BlockSpec(
        (None, BLOCK, n), lambda b, i: (b, jnp.maximum(i * (TQ // BLOCK) - 1, 0), 0))
    out = pl.pallas_call(
        _swa_body,
        grid=(batch, seq_len // TQ),
        in_specs=[cur(PW_Q), cur(PW_KD), prev(PW_KD), cur(PW_VD), prev(PW_VD),
                  _resident(bias.shape, (layer,))],
        out_specs=cur(ATTN_WIDTH),
        out_shape=jax.ShapeDtypeStruct((batch, seq_len, ATTN_WIDTH), BF16),
        compiler_params=_params("parallel", "parallel"),
        name="swa",
    )(q3, kd3, kd3, vd3, vd3, bias)
    return out.reshape(batch * seq_len, ATTN_WIDTH)


def _split2(x):
    hi = x.astype(BF16)
    return hi, (x - hi.astype(F32)).astype(BF16)


def _wkv_body(r_ref, k_ref, v_ref, lw_ref, a_ref, g_ref,
              kk_ref, ka_ref, rk_ref, gng_ref, gnb_ref, o_ref, s_ref):
    t = pl.program_id(1)
    tblk, width = r_ref.shape
    n_pairs = width // PAIR

    @pl.when(t == 0)
    def _():
        s_ref[...] = jnp.zeros(s_ref.shape, F32)

    lane = lax.broadcasted_iota(jnp.int32, (1, PAIR), 1)
    lo = lane < RW_HEAD
    ri = lax.broadcasted_iota(jnp.int32, (PAIR, PAIR), 0)
    ci = lax.broadcasted_iota(jnp.int32, (PAIR, PAIR), 1)
    same_head = (ri >= RW_HEAD) == (ci >= RW_HEAD)
    head_ones = jnp.where(same_head, 1.0, 0.0).astype(BF16)
    eye = ri == ci
    tok_r = lax.broadcasted_iota(jnp.int32, (CHUNK, PAIR), 0)
    tok_c = lax.broadcasted_iota(jnp.int32, (CHUNK, PAIR), 1) & (CHUNK - 1)
    strict, incl, diag = tok_c < tok_r, tok_c <= tok_r, tok_c == tok_r
    rt_ = lax.broadcasted_iota(jnp.int32, (MXU_DIM, MXU_DIM), 0)
    ct_ = lax.broadcasted_iota(jnp.int32, (MXU_DIM, MXU_DIM), 1)
    same_chunk = (rt_ & -CHUNK) == (ct_ & -CHUNK)
    chunk_tril = jnp.where(jnp.logical_and(same_chunk, ct_ <= rt_), 1.0, 0.0).astype(BF16)

    def seg_sum(x):
        xb = x.astype(BF16)
        return jnp.concatenate(
            [_dot(xb[:, p * PAIR:(p + 1) * PAIR], head_ones) for p in range(n_pairs)], axis=1)

    def stack(x):
        x = x.astype(BF16)
        zero = jnp.zeros((), BF16)
        return jnp.concatenate([jnp.where(lo, x, zero), jnp.where(lo, zero, x)], axis=0)

    def per_head(x):
        return jnp.where(jnp.concatenate([same_head] * (x.shape[1] // PAIR), axis=1), x, 0.0)

    r, k, v, lw, a = r_ref[...], k_ref[...], v_ref[...], lw_ref[...], a_ref[...]
    kk = k * kk_ref[...]
    kk = kk / jnp.maximum(jnp.sqrt(seg_sum(kk * kk)), 1e-12)
    k = k * (1.0 + (a - 1.0) * ka_ref[...])
    b = kk * a
    bonus = seg_sum(r * k * rk_ref[...]) * v

    lw_hi, lw_lo = _split2(lw)
    cum = jnp.concatenate(
        [_dot(chunk_tril, lw_hi[i:i + MXU_DIM]) + _dot(chunk_tril, lw_lo[i:i + MXU_DIM])
         for i in range(0, tblk, MXU_DIM)], axis=0)

    n_chunks = tblk // CHUNK
    units = [(p, c) for c in range(n_chunks) for p in range(n_pairs)]
    idx = range(len(units))
    n_levels = int(math.log2(CHUNK))
    at, rt, lhs, rhs, bh, kh, vd, vs, dec = [], [], [], [], [], [], [], [], []
    for p, c in units:
        rows, cols = slice(c * CHUNK, (c + 1) * CHUNK), slice(p * PAIR, (p + 1) * PAIR)
        cu, lwu = cum[rows, cols], lw[rows, cols]
        ru, ku, bu, vu = r[rows, cols], k[rows, cols], b[rows, cols], v[rows, cols]
        end = cu[CHUNK - 1:CHUNK, :]
        e_pos, e_neg = jnp.exp(cu), jnp.exp(-cu)
        e_prev, e_end = jnp.exp(cu - lwu), jnp.exp(end - cu)
        at.append(-kk[rows, cols] * e_prev)
        rt.append(ru * e_pos)
        lhs.append(jnp.concatenate([at[-1], rt[-1]], axis=0).astype(BF16))
        rhs.append(jnp.concatenate([stack(bu * e_neg), stack(ku * e_neg)], axis=0))
        bh.append(jnp.transpose(bu * e_end).astype(BF16))
        kh.append(jnp.transpose(ku * e_end).astype(BF16))
        vd.append(vu.astype(BF16))
        vs.append(stack(vu))
        dec.append(jnp.where(eye, jnp.exp(end), 0.0))

    g = [_dot_nt(lhs[u], rhs[u]) for u in idx]
    a_ab = [jnp.where(strict, g[u][:CHUNK, :PAIR], 0.0) for u in idx]
    a_rb = [jnp.where(incl, g[u][CHUNK:, :PAIR], 0.0).astype(BF16) for u in idx]
    a_kv = [jnp.concatenate([jnp.where(strict, g[u][:CHUNK, PAIR:], 0.0),
                             jnp.where(incl, g[u][CHUNK:, PAIR:], 0.0)], axis=0).astype(BF16) for u in idx]
    akv = [_dot(a_kv[u], vs[u]) for u in idx]
    khv = [_dot(kh[u], vd[u]) for u in idx]

    inv = [jnp.where(diag, 1.0, a_ab[u]) for u in idx]
    pw = [a_ab[u].astype(BF16) for u in idx]
    pw = [_dot(pw[u], stack(pw[u])).astype(BF16) for u in idx]
    for level in range(1, n_levels):
        if level + 1 < n_levels:
            both = [_dot(pw[u], jnp.concatenate([stack(inv[u]), stack(pw[u])], axis=1)) for u in idx]
            inv = [inv[u] + both[u][:, :PAIR] for u in idx]
            pw = [both[u][:, PAIR:].astype(BF16) for u in idx]
        else:
            inv = [inv[u] + _dot(pw[u], stack(inv[u])) for u in idx]
    x = [_dot(inv[u].astype(BF16),
              jnp.concatenate([stack(at[u]), stack(akv[u][:CHUNK])], axis=1)).astype(BF16) for u in idx]

    qy = [_dot(a_rb[u], jnp.concatenate([stack(x[u][:, :PAIR]), stack(x[u][:, PAIR:])], axis=1))
          + jnp.concatenate([rt[u], akv[u][CHUNK:]], axis=1) for u in idx]
    mc = [per_head(_dot(bh[u], x[u])) + jnp.concatenate([dec[u], per_head(khv[u])], axis=1)
          for u in idx]
    q_hat = [qy[u][:, :PAIR].astype(BF16) for u in idx]
    m_hat = [mc[u][:, :PAIR].astype(BF16) for u in idx]

    state = [s_ref[p] for p in range(n_pairs)]
    ys = [[None] * n_chunks for _ in range(n_pairs)]
    for u, (p, c) in enumerate(units):
        sb = state[p].astype(BF16)
        ys[p][c] = _dot(q_hat[u], sb) + qy[u][:, PAIR:]
        state[p] = _dot(m_hat[u], sb) + mc[u][:, PAIR:]
    for p in range(n_pairs):
        s_ref[p] = state[p]

    y = jnp.concatenate([jnp.concatenate(ys[p], axis=0) for p in range(n_pairs)], axis=1)
    inv_n = 1.0 / RW_HEAD
    d = y - seg_sum(y) * inv_n
    var = seg_sum(d * d) * inv_n
    y = d * lax.rsqrt(var + GN_EPS) * gng_ref[...] + gnb_ref[...]
    o_ref[...] = ((y + bonus) * g_ref[...].astype(F32)).astype(BF16)


def _wkv7(r, k, v, lw, a, g, k_k, k_a, r_k, gn_g, gn_b, batch, seq_len):
    tok = pl.BlockSpec((None, WKV_TBLK, RW_WIDTH), lambda b, t: (b, t, 0))
    shape3 = (batch, seq_len, RW_WIDTH)
    params = (k_k, k_a, r_k, gn_g, gn_b)
    out = pl.pallas_call(
        _wkv_body,
        grid=(batch, seq_len // WKV_TBLK),
        in_specs=[tok] * 6 + [_resident(t.shape) for t in params],
        out_specs=tok,
        out_shape=jax.ShapeDtypeStruct(shape3, BF16),
        scratch_shapes=[pltpu.VMEM((N_PAIRS, PAIR, PAIR), F32)],
        compiler_params=_params("parallel", "arbitrary"),
        name="wkv7",
    )(*(t.reshape(shape3) for t in (r, k, v, lw, a, g)), *params)
    return out.reshape(batch * seq_len, RW_WIDTH)


def _mix_ln_body(attn_ref, rw_ref, gab_ref, x_ref, wba_ref, wbb_ref, wo_ref, g_ref, b_ref, o_ref):
    half = x_ref.shape[0] // 2
    for part in range(2):
        rows = slice(part * half, (part + 1) * half)
        gab = gab_ref[rows, :]
        merged = (gab[:, :D_MODEL].astype(F32) * _dot(attn_ref[rows, :], wba_ref[...])
                  + gab[:, D_MODEL:].astype(F32) * _dot(rw_ref[rows, :], wbb_ref[...]))
        mix = _dot(merged.astype(BF16), wo_ref[...])
        o_ref[rows, :] = _layer_norm(ALPHA * x_ref[rows, :] + mix, g_ref[...], b_ref[...])


def _mix_ln(attn, rw, gab, x, layer, wba, wbb, wo, g, b):
    m = x.shape[0]
    tile = lambda n: pl.BlockSpec((TM_MIX, n), lambda i: (i, 0))
    return pl.pallas_call(
        _mix_ln_body,
        grid=(m // TM_MIX,),
        in_specs=[tile(ATTN_WIDTH), tile(RW_WIDTH), tile(PW_GATES), tile(D_MODEL)]
        + [_resident(t.shape, (layer,)) for t in (wba, wbb, wo)] + [_resident(g.shape), _resident(b.shape)],
        out_specs=tile(D_MODEL),
        out_shape=jax.ShapeDtypeStruct((m, D_MODEL), F32),
        compiler_params=_params("parallel"),
        name="mix_ln",
    )(attn, rw, gab, x, wba, wbb, wo, g, b)


def _t5_bucket(n):
    max_exact = N_BUCKETS // 2
    nf = jnp.maximum(n, 1).astype(F32)
    large = max_exact + (jnp.log(nf / max_exact) / math.log(MAX_DISTANCE / max_exact)
                         * (N_BUCKETS - max_exact)).astype(jnp.int32)
    large = jnp.minimum(large, N_BUCKETS - 1)
    return jnp.where(n < max_exact, n, large)


def _attention_bias(rel_bias, sinks):
    dist_bias = rel_bias[_t5_bucket(jnp.arange(WINDOW, dtype=jnp.int32))].T
    period = 3 * BLOCK
    neg = lambda n: jnp.full((ATTN_Q_HEADS, n), -jnp.inf, F32)
    row0 = jnp.concatenate([neg(1), dist_bias[:, ::-1].astype(F32), neg(period - 1 - WINDOW)], axis=1)
    bias = jnp.tile(row0, (1, BLOCK))[:, :BLOCK * (period - 1)]
    bias = bias.reshape(ATTN_Q_HEADS, BLOCK, period - 1)[:, :, :2 * BLOCK]
    key = jnp.arange(2 * BLOCK)
    bias = jnp.where(key == 0, sinks.astype(F32)[:, :, None, None], bias[None])
    return bias.reshape(sinks.shape[0], ATTN_KV_HEADS, ATTN_GROUP * BLOCK, 2 * BLOCK)


def kernel(x, ffn_w_gu, ffn_w_down, ln_g, ln_b, w_in, b_in, rel_bias, attn_sinks, shift_mu,
           rw_w0, rw_w2, rw_a0, rw_a2, rw_g2, rw_k_k, rw_k_a, rw_r_k, rw_gn_g, rw_gn_b,
           rw_v0, rw_v1, rw_v2, w_branch_attn, w_branch_rwkv, w_out):
    batch, seq_len, _ = x.shape
    depth = w_in.shape[0]
    assert all(seq_len % n == 0 for n in (TM, TM_FFN, TM_MIX, TQ, WKV_TBLK))
    assert WKV_TBLK % MXU_DIM == 0
    row = lambda t: t.reshape(1, -1)
    bf = lambda t: t.astype(BF16)
    wgu, wd, w_proj = bf(ffn_w_gu), bf(ffn_w_down), bf(w_in)
    wba, wbb, wo = bf(w_branch_attn), bf(w_branch_rwkv), bf(w_out)
    bias = _attention_bias(rel_bias, attn_sinks)
    h = x.reshape(batch * seq_len, D_MODEL)
    v_first = None
    for l in range(depth):
        h = _ffn_ln(h, wgu, wd, (l, 0), row(ln_g[l, 0]), row(ln_b[l, 0]))
        vmix = None if l == 0 else (v_first, row(rw_v0[l - 1]), bf(rw_v1[l - 1]), bf(rw_v2[l - 1]))
        gab, q, kd, vd, r, k, v, lw, a, g = _proj_prep(
            h, seq_len, l, w_proj, row(b_in[l]), row(shift_mu[l]), row(rw_w0[l]), bf(rw_w2[l]),
            row(rw_a0[l]), bf(rw_a2[l]), bf(rw_g2[l]), vmix)
        if l == 0:
            v_first = v
        attn = _swa(q, kd, vd, bias, l, batch, seq_len)
        rw = _wkv7(r, k, v, lw, a, g, row(rw_k_k[l]), row(rw_k_a[l]), row(rw_r_k[l]),
                   row(rw_gn_g[l]), row(rw_gn_b[l]), batch, seq_len)
        h = _mix_ln(attn, rw, gab, h, l, wba, wbb, wo, row(ln_g[l, 1]), row(ln_b[l, 1]))
        h = _ffn_ln(h, wgu, wd, (l, 1), row(ln_g[l, 2]), row(ln_b[l, 2]))
    return h.reshape(batch, seq_len, D_MODEL)
```

```python
import functools
import math

import jax
import jax.numpy as jnp
from jax import lax
from jax.experimental import pallas as pl
from jax.experimental.pallas import tpu as pltpu

F32 = jnp.float32
BF16 = jnp.bfloat16

D_MODEL = 1024
DEPTH = 2
HEAD_DIM = 64
ATTN_Q_HEADS = 8
ATTN_KV_HEADS = 2
ATTN_GROUP = ATTN_Q_HEADS // ATTN_KV_HEADS
ATTN_WIDTH = ATTN_Q_HEADS * HEAD_DIM
KV_WIDTH = ATTN_KV_HEADS * HEAD_DIM
WINDOW = 128
BLOCK = 128
N_BUCKETS = 32
MAX_DISTANCE = 128
RW_HEADS = 8
RW_HEAD = 64
RW_WIDTH = RW_HEADS * RW_HEAD
DECAY_RANK = 64
A_RANK = 64
V_RANK = 32
G_RANK = 128
GN_EPS = 64e-5
D_FF = 2816
ALPHA = (2 * DEPTH) ** 0.25
LN_EPS = 1e-5
RW_COLS = 3 * RW_WIDTH + DECAY_RANK + A_RANK + G_RANK

LANES = 128
MXU_DIM = 256
VMEM_LIMIT = 56 * 1024 * 1024

TM = 512
TM_FFN = 1024
TM_MIX = 1024
FF_CHUNK = MXU_DIM
N_FF_CHUNKS = D_FF // FF_CHUNK
TQ = 1024
CHUNK = 64
WKV_TBLK = 512
PAIR = 2 * RW_HEAD
N_PAIRS = RW_WIDTH // PAIR

PW_GATES = 2 * D_MODEL
PW_Q = ATTN_WIDTH
PW_KD = 2 * KV_WIDTH
PW_VD = 2 * KV_WIDTH
PO_Q = PW_GATES
PO_RW = PO_Q + ATTN_WIDTH + 2 * KV_WIDTH
PROJ_COLS = PO_RW + RW_COLS


def _dot(a, b):
    return jnp.dot(a, b, preferred_element_type=F32)


def _dot_nt(a, b):
    return lax.dot_general(a, b, (((1,), (1,)), ((), ())), preferred_element_type=F32)


def _layer_norm(y, g, b):
    mu = jnp.mean(y, axis=-1, keepdims=True)
    d = y - mu
    var = jnp.mean(d * d, axis=-1, keepdims=True)
    return d * lax.rsqrt(var + LN_EPS) * g + b


def _resident(shape, lead=()):
    rest = len(shape) - len(lead)
    block = (None,) * len(lead) + tuple(shape[len(lead):])
    return pl.BlockSpec(block, lambda *_: tuple(lead) + (0,) * rest, pipeline_mode=pl.Buffered(1))


def _params(*sem):
    return pltpu.CompilerParams(dimension_semantics=sem, vmem_limit_bytes=VMEM_LIMIT)


def _ffn_ln_body(x_ref, wgu_ref, wd_ref, g_ref, b_ref, o_ref):
    x = x_ref[...]
    xb = x.astype(BF16)
    acc = jnp.zeros(x.shape, F32)
    for j in range(N_FF_CHUNKS):
        cols = slice(j * FF_CHUNK, (j + 1) * FF_CHUNK)
        gate = _dot(xb, wgu_ref[:, cols])
        up = _dot(xb, wgu_ref[:, D_FF + j * FF_CHUNK:D_FF + (j + 1) * FF_CHUNK])
        h = jax.nn.silu(gate) * up
        acc = acc + _dot(h.astype(BF16), wd_ref[cols, :])
    o_ref[...] = _layer_norm(ALPHA * x + 0.5 * acc, g_ref[...], b_ref[...])


def _ffn_ln(x, wgu, wd, lead, g, b):
    m = x.shape[0]
    return pl.pallas_call(
        _ffn_ln_body,
        grid=(m // TM_FFN,),
        in_specs=[pl.BlockSpec((TM_FFN, D_MODEL), lambda i: (i, 0)),
                  _resident(wgu.shape, lead), _resident(wd.shape, lead),
                  _resident(g.shape), _resident(b.shape)],
        out_specs=pl.BlockSpec((TM_FFN, D_MODEL), lambda i: (i, 0)),
        out_shape=jax.ShapeDtypeStruct((m, D_MODEL), F32),
        compiler_params=_params("parallel"),
        name="ffn_ln",
    )(x, wgu, wd, g, b)


def _softplus(z):
    return jnp.maximum(z, 0.0) + jnp.log1p(jnp.exp(-jnp.abs(z)))


def _proj_body(tiles_per_seq, has_vmix, *refs):
    (x_ref, w_ref, b_ref, mu_ref, w0_ref, w2_ref, a0_ref, a2_ref, g2_ref) = refs[:9]
    refs = refs[9:]
    if has_vmix:
        vf_ref, v0_ref, v1_ref, v2_ref = refs[:4]
        refs = refs[4:]
    (gab_ref, q_ref, kd_ref, vd_ref, r_ref, k_ref, v_ref, lw_ref, a_ref, g_ref, carry_ref) = refs

    i = pl.program_id(0)
    tm = x_ref.shape[0]

    @pl.when(i % tiles_per_seq == 0)
    def _():
        carry_ref[...] = jnp.zeros(carry_ref.shape, F32)

    xb = x_ref[...].astype(BF16)
    part = lambda lo, hi: _dot(xb, w_ref[:, lo:hi]) + b_ref[:, lo:hi]
    u = part(PO_RW, PROJ_COLS)
    gates = part(0, PW_GATES)

    before_tile = carry_ref[7:8, :]
    carry_ref[...] = u[tm - 8:, :]
    row = lax.broadcasted_iota(jnp.int32, (tm, 1), 0)
    u_prev = jnp.where(row == 0, before_tile, pltpu.roll(u, 1, 0))
    u = u + (u_prev - u) * mu_ref[...]

    c = RW_WIDTH
    r = u[:, :c]
    k = u[:, c:2 * c]
    v = u[:, 2 * c:3 * c]
    o = 3 * c
    xw = u[:, o:o + DECAY_RANK]
    xa = u[:, o + DECAY_RANK:o + DECAY_RANK + A_RANK]
    xg = u[:, o + DECAY_RANK + A_RANK:]

    w = -_softplus(-(w0_ref[...] + _dot(jnp.tanh(xw).astype(BF16), w2_ref[...]))) - 0.5
    lw_ref[...] = -jnp.exp(w)
    a_ref[...] = jax.nn.sigmoid(a0_ref[...] + _dot(xa.astype(BF16), a2_ref[...])).astype(BF16)
    g_ref[...] = _dot(jax.nn.sigmoid(xg).astype(BF16), g2_ref[...]).astype(BF16)
    if has_vmix:
        low = _dot(v.astype(BF16), v1_ref[...])
        mix = jax.nn.sigmoid(v0_ref[...] + _dot(low.astype(BF16), v2_ref[...]))
        v = v + (vf_ref[...] - v) * mix
    r_ref[...] = r.astype(BF16)
    k_ref[...] = k.astype(BF16)
    v_ref[...] = v.astype(BF16)

    qkv = part(PO_Q, PO_RW)
    gab_ref[...] = jax.nn.sigmoid(gates).astype(BF16)
    q_ref[...] = (qkv[:, :ATTN_WIDTH] * (HEAD_DIM ** -0.5)).astype(BF16)
    lo = lax.broadcasted_iota(jnp.int32, (1, LANES), 1) < HEAD_DIM
    for ref, at in ((kd_ref, ATTN_WIDTH), (vd_ref, ATTN_WIDTH + KV_WIDTH)):
        pair = qkv[:, at:at + KV_WIDTH]
        swapped = pltpu.roll(pair, HEAD_DIM, 1)
        ref[...] = jnp.concatenate([jnp.where(lo, pair, swapped),
                                    jnp.where(lo, swapped, pair)], axis=1).astype(BF16)


def _proj_prep(x, seq_len, layer, w, b, mu, w0, w2, a0, a2, g2, vmix):
    m = x.shape[0]
    tile = lambda n: pl.BlockSpec((TM, n), lambda i: (i, 0))
    has_vmix = vmix is not None
    args = [x, w, b, mu, w0, w2, a0, a2, g2]
    in_specs = [tile(D_MODEL), _resident(w.shape, (layer,))] + [_resident(t.shape) for t in args[2:]]
    if has_vmix:
        vf, v0, v1, v2 = vmix
        args += [vf, v0, v1, v2]
        in_specs += [tile(RW_WIDTH), _resident(v0.shape), _resident(v1.shape), _resident(v2.shape)]
    widths = [(PW_GATES, BF16), (PW_Q, BF16), (PW_KD, BF16), (PW_VD, BF16)] \
        + [(RW_WIDTH, BF16)] * 3 + [(RW_WIDTH, F32)] + [(RW_WIDTH, BF16)] * 2
    return pl.pallas_call(
        functools.partial(_proj_body, seq_len // TM, has_vmix),
        grid=(m // TM,),
        in_specs=in_specs,
        out_specs=[tile(n) for n, _ in widths],
        out_shape=[jax.ShapeDtypeStruct((m, n), dt) for n, dt in widths],
        scratch_shapes=[pltpu.VMEM((8, RW_COLS), F32)],
        compiler_params=_params("arbitrary"),
        name="proj_prep",
    )(*args)


def _swa_body(q_ref, kc_ref, kp_ref, vc_ref, vp_ref, bias_ref, o_ref):
    i = pl.program_id(1)
    lane = lax.broadcasted_iota(jnp.int32, (1, LANES), 1)
    lo = lane < HEAD_DIM
    key = lax.broadcasted_iota(jnp.int32, (1, 2 * BLOCK), 1)
    not_key0 = lax.broadcasted_iota(jnp.int32, (2 * BLOCK, 1), 0) > 0
    no_prev = jnp.logical_and(i == 0, jnp.logical_and(key < BLOCK, key > 0))
    ones = jnp.ones((2 * BLOCK, LANES), BF16)
    zero = jnp.zeros((), BF16)
    units = [(j, h) for j in range(TQ // BLOCK) for h in range(ATTN_KV_HEADS)]
    idx = range(len(units))
    scores, v_lo, v_hi = [], [], []
    for j, h in units:
        rows = slice(j * BLOCK, (j + 1) * BLOCK)
        cols = slice(h * LANES, (h + 1) * LANES)
        if j == 0:
            k_prev, v_prev = kp_ref[:, cols], vp_ref[:, cols]
        else:
            before = slice((j - 1) * BLOCK, j * BLOCK)
            k_prev, v_prev = kc_ref[before, cols], vc_ref[before, cols]
        kb = jnp.concatenate([k_prev, kc_ref[rows, cols]], axis=0)
        vb = jnp.concatenate([v_prev, vc_ref[rows, cols]], axis=0)
        kb, vb = jnp.where(not_key0, kb, zero), jnp.where(not_key0, vb, zero)
        parts = []
        for p in (2 * h, 2 * h + 1):
            qp = q_ref[rows, p * LANES:(p + 1) * LANES]
            parts += [jnp.where(lo, qp, zero), jnp.where(lo, zero, qp)]
        s = _dot_nt(jnp.concatenate(parts, axis=0), kb) + bias_ref[h]
        if j == 0:
            s = jnp.where(no_prev, -jnp.inf, s)
        scores.append(s)
        v_lo.append(jnp.where(lo, vb, zero))
        v_hi.append(jnp.where(lo, zero, vb))
    mx = [jnp.max(scores[u], axis=-1, keepdims=True) for u in idx]
    eb = [jnp.exp(scores[u] - mx[u]).astype(BF16) for u in idx]
    den = [_dot(eb[u], ones) for u in idx]
    for u, (j, h) in enumerate(units):
        rows = slice(j * BLOCK, (j + 1) * BLOCK)
        for n, p in enumerate((2 * h, 2 * h + 1)):
            top = slice(2 * n * BLOCK, (2 * n + 1) * BLOCK)
            bot = slice((2 * n + 1) * BLOCK, (2 * n + 2) * BLOCK)
            out = (_dot(eb[u][top], v_lo[u]) / den[u][top]
                   + _dot(eb[u][bot], v_hi[u]) / den[u][bot])
            o_ref[rows, p * LANES:(p + 1) * LANES] = out.astype(BF16)


def _swa(q, kd, vd, bias, layer, batch, seq_len):
    q3 = q.reshape(batch, seq_len, PW_Q)
    kd3 = kd.reshape(batch, seq_len, PW_KD)
    vd3 = vd.reshape(batch, seq_len, PW_VD)
    cur = lambda n: pl.BlockSpec((None, TQ, n), lambda b, i: (b, i, 0))
    prev = lambda n: pl.BlockSpec(
        (None, BLOCK, n), lambda b, i: (b, jnp.maximum(i * (TQ // BLOCK) - 1, 0), 0))
    out = pl.pallas_call(
        _swa_body,
        grid=(batch, seq_len // TQ),
        in_specs=[cur(PW_Q), cur(PW_KD), prev(PW_KD), cur(PW_VD), prev(PW_VD),
                  _resident(bias.shape, (layer,))],
        out_specs=cur(ATTN_WIDTH),
        out_shape=jax.ShapeDtypeStruct((batch, seq_len, ATTN_WIDTH), BF16),
        compiler_params=_params("parallel", "parallel"),
        name="swa",
    )(q3, kd3, kd3, vd3, vd3, bias)
    return out.reshape(batch * seq_len, ATTN_WIDTH)


def _split2(x):
    hi = x.astype(BF16)
    return hi, (x - hi.astype(F32)).astype(BF16)


def _wkv_body(r_ref, k_ref, v_ref, lw_ref, a_ref, g_ref,
              kk_ref, ka_ref, rk_ref, gng_ref, gnb_ref, o_ref, s_ref):
    t = pl.program_id(1)
    tblk, width = r_ref.shape
    n_pairs = width // PAIR

    @pl.when(t == 0)
    def _():
        s_ref[...] = jnp.zeros(s_ref.shape, F32)

    lane = lax.broadcasted_iota(jnp.int32, (1, PAIR), 1)
    lo = lane < RW_HEAD
    ri = lax.broadcasted_iota(jnp.int32, (PAIR, PAIR), 0)
    ci = lax.broadcasted_iota(jnp.int32, (PAIR, PAIR), 1)
    same_head = (ri >= RW_HEAD) == (ci >= RW_HEAD)
    head_ones = jnp.where(same_head, 1.0, 0.0).astype(BF16)
    eye = ri == ci
    tok_r = lax.broadcasted_iota(jnp.int32, (CHUNK, PAIR), 0)
    tok_c = lax.broadcasted_iota(jnp.int32, (CHUNK, PAIR), 1) & (CHUNK - 1)
    strict, incl, diag = tok_c < tok_r, tok_c <= tok_r, tok_c == tok_r
    rt_ = lax.broadcasted_iota(jnp.int32, (MXU_DIM, MXU_DIM), 0)
    ct_ = lax.broadcasted_iota(jnp.int32, (MXU_DIM, MXU_DIM), 1)
    same_chunk = (rt_ & -CHUNK) == (ct_ & -CHUNK)
    chunk_tril = jnp.where(jnp.logical_and(same_chunk, ct_ <= rt_), 1.0, 0.0).astype(BF16)

    def seg_sum(x):
        xb = x.astype(BF16)
        return jnp.concatenate(
            [_dot(xb[:, p * PAIR:(p + 1) * PAIR], head_ones) for p in range(n_pairs)], axis=1)

    def stack(x):
        x = x.astype(BF16)
        zero = jnp.zeros((), BF16)
        return jnp.concatenate([jnp.where(lo, x, zero), jnp.where(lo, zero, x)], axis=0)

    def per_head(x):
        return jnp.where(jnp.concatenate([same_head] * (x.shape[1] // PAIR), axis=1), x, 0.0)

    r, k, v, a = (ref[...].astype(F32) for ref in (r_ref, k_ref, v_ref, a_ref))
    lw = lw_ref[...]
    kk = k * kk_ref[...]
    kk = kk / jnp.maximum(jnp.sqrt(seg_sum(kk * kk)), 1e-12)
    k = k * (1.0 + (a - 1.0) * ka_ref[...])
    b = kk * a
    bonus = seg_sum(r * k * rk_ref[...]) * v

    lw_hi, lw_lo = _split2(lw)
    cum = jnp.concatenate(
        [_dot(chunk_tril, lw_hi[i:i + MXU_DIM]) + _dot(chunk_tril, lw_lo[i:i + MXU_DIM])
         for i in range(0, tblk, MXU_DIM)], axis=0)

    n_chunks = tblk // CHUNK
    units = [(p, c) for c in range(n_chunks) for p in range(n_pairs)]
    idx = range(len(units))
    n_levels = int(math.log2(CHUNK))
    at, rt, lhs, rhs, bh, kh, vd, vs, dec = [], [], [], [], [], [], [], [], []
    for p, c in units:
        rows, cols = slice(c * CHUNK, (c + 1) * CHUNK), slice(p * PAIR, (p + 1) * PAIR)
        cu, lwu = cum[rows, cols], lw[rows, cols]
        ru, ku, bu, vu = r[rows, cols], k[rows, cols], b[rows, cols], v[rows, cols]
        end = cu[CHUNK - 1:CHUNK, :]
        e_pos, e_neg = jnp.exp(cu), jnp.exp(-cu)
        e_prev, e_end = jnp.exp(cu - lwu), jnp.exp(end - cu)
        at.append(-kk[rows, cols] * e_prev)
        rt.append(ru * e_pos)
        lhs.append(jnp.concatenate([at[-1], rt[-1]], axis=0).astype(BF16))
        rhs.append(jnp.concatenate([stack(bu * e_neg), stack(ku * e_neg)], axis=0))
        bh.append(jnp.transpose(bu * e_end).astype(BF16))
        kh.append(jnp.transpose(ku * e_end).astype(BF16))
        vd.append(vu.astype(BF16))
        vs.append(stack(vu))
        dec.append(jnp.where(eye, jnp.exp(end), 0.0))

    g = [_dot_nt(lhs[u], rhs[u]) for u in idx]
    a_ab = [jnp.where(strict, g[u][:CHUNK, :PAIR], 0.0) for u in idx]
    a_rb = [jnp.where(incl, g[u][CHUNK:, :PAIR], 0.0).astype(BF16) for u in idx]
    a_kv = [jnp.concatenate([jnp.where(strict, g[u][:CHUNK, PAIR:], 0.0),
                             jnp.where(incl, g[u][CHUNK:, PAIR:], 0.0)], axis=0).astype(BF16) for u in idx]
    akv = [_dot(a_kv[u], vs[u]) for u in idx]
    khv = [_dot(kh[u], vd[u]) for u in idx]

    inv = [jnp.where(diag, 1.0, a_ab[u]) for u in idx]
    pw = [a_ab[u].astype(BF16) for u in idx]
    pw = [_dot(pw[u], stack(pw[u])).astype(BF16) for u in idx]
    for level in range(1, n_levels):
        if level + 1 < n_levels:
            both = [_dot(pw[u], jnp.concatenate([stack(inv[u]), stack(pw[u])], axis=1)) for u in idx]
            inv = [inv[u] + both[u][:, :PAIR] for u in idx]
            pw = [both[u][:, PAIR:].astype(BF16) for u in idx]
        else:
            inv = [inv[u] + _dot(pw[u], stack(inv[u])) for u in idx]
    x = [_dot(inv[u].astype(BF16),
              jnp.concatenate([stack(at[u]), stack(akv[u][:CHUNK])], axis=1)).astype(BF16) for u in idx]

    qy = [_dot(a_rb[u], jnp.concatenate([stack(x[u][:, :PAIR]), stack(x[u][:, PAIR:])], axis=1))
          + jnp.concatenate([rt[u], akv[u][CHUNK:]], axis=1) for u in idx]
    mc = [per_head(_dot(bh[u], x[u])) + jnp.concatenate([dec[u], per_head(khv[u])], axis=1)
          for u in idx]
    q_hat = [qy[u][:, :PAIR].astype(BF16) for u in idx]
    m_hat = [mc[u][:, :PAIR].astype(BF16) for u in idx]

    state = [s_ref[p] for p in range(n_pairs)]
    ys = [[None] * n_chunks for _ in range(n_pairs)]
    for u, (p, c) in enumerate(units):
        sb = state[p].astype(BF16)
        ys[p][c] = _dot(q_hat[u], sb) + qy[u][:, PAIR:]
        state[p] = _dot(m_hat[u], sb) + mc[u][:, PAIR:]
    for p in range(n_pairs):
        s_ref[p] = state[p]

    y = jnp.concatenate([jnp.concatenate(ys[p], axis=0) for p in range(n_pairs)], axis=1)
    inv_n = 1.0 / RW_HEAD
    d = y - seg_sum(y) * inv_n
    var = seg_sum(d * d) * inv_n
    y = d * lax.rsqrt(var + GN_EPS) * gng_ref[...] + gnb_ref[...]
    o_ref[...] = ((y + bonus) * g_ref[...].astype(F32)).astype(BF16)


def _wkv7(r, k, v, lw, a, g, k_k, k_a, r_k, gn_g, gn_b, batch, seq_len):
    tok = pl.BlockSpec((None, WKV_TBLK, RW_WIDTH), lambda b, t: (b, t, 0))
    shape3 = (batch, seq_len, RW_WIDTH)
    params = (k_k, k_a, r_k, gn_g, gn_b)
    out = pl.pallas_call(
        _wkv_body,
        grid=(batch, seq_len // WKV_TBLK),
        in_specs=[tok] * 6 + [_resident(t.shape) for t in params],
        out_specs=tok,
        out_shape=jax.ShapeDtypeStruct(shape3, BF16),
        scratch_shapes=[pltpu.VMEM((N_PAIRS, PAIR, PAIR), F32)],
        compiler_params=_params("parallel", "arbitrary"),
        name="wkv7",
    )(*(t.reshape(shape3) for t in (r, k, v, lw, a, g)), *params)
    return out.reshape(batch * seq_len, RW_WIDTH)


def _mix_ln_body(attn_ref, rw_ref, gab_ref, x_ref, wba_ref, wbb_ref, wo_ref, g_ref, b_ref, o_ref):
    half = x_ref.shape[0] // 2
    for part in range(2):
        rows = slice(part * half, (part + 1) * half)
        gab = gab_ref[rows, :]
        merged = (gab[:, :D_MODEL].astype(F32) * _dot(attn_ref[rows, :], wba_ref[...])
                  + gab[:, D_MODEL:].astype(F32) * _dot(rw_ref[rows, :], wbb_ref[...]))
        mix = _dot(merged.astype(BF16), wo_ref[...])
        o_ref[rows, :] = _layer_norm(ALPHA * x_ref[rows, :] + mix, g_ref[...], b_ref[...])


def _mix_ln(attn, rw, gab, x, layer, wba, wbb, wo, g, b):
    m = x.shape[0]
    tile = lambda n: pl.BlockSpec((TM_MIX, n), lambda i: (i, 0))
    return pl.pallas_call(
        _mix_ln_body,
        grid=(m // TM_MIX,),
        in_specs=[tile(ATTN_WIDTH), tile(RW_WIDTH), tile(PW_GATES), tile(D_MODEL)]
        + [_resident(t.shape, (layer,)) for t in (wba, wbb, wo)] + [_resident(g.shape), _resident(b.shape)],
        out_specs=tile(D_MODEL),
        out_shape=jax.ShapeDtypeStruct((m, D_MODEL), F32),
        compiler_params=_params("parallel"),
        name="mix_ln",
    )(attn, rw, gab, x, wba, wbb, wo, g, b)


def _t5_bucket(n):
    max_exact = N_BUCKETS // 2
    nf = jnp.maximum(n, 1).astype(F32)
    large = max_exact + (jnp.log(nf / max_exact) / math.log(MAX_DISTANCE / max_exact)
                         * (N_BUCKETS - max_exact)).astype(jnp.int32)
    large = jnp.minimum(large, N_BUCKETS - 1)
    return jnp.where(n < max_exact, n, large)


def _attention_bias(rel_bias, sinks):
    dist_bias = rel_bias[_t5_bucket(jnp.arange(WINDOW, dtype=jnp.int32))].T
    period = 3 * BLOCK
    neg = lambda n: jnp.full((ATTN_Q_HEADS, n), -jnp.inf, F32)
    row0 = jnp.concatenate([neg(1), dist_bias[:, ::-1].astype(F32), neg(period - 1 - WINDOW)], axis=1)
    bias = jnp.tile(row0, (1, BLOCK))[:, :BLOCK * (period - 1)]
    bias = bias.reshape(ATTN_Q_HEADS, BLOCK, period - 1)[:, :, :2 * BLOCK]
    key = jnp.arange(2 * BLOCK)
    bias = jnp.where(key == 0, sinks.astype(F32)[:, :, None, None], bias[None])
    return bias.reshape(sinks.shape[0], ATTN_KV_HEADS, ATTN_GROUP * BLOCK, 2 * BLOCK)


def kernel(x, ffn_w_gu, ffn_w_down, ln_g, ln_b, w_in, b_in, rel_bias, attn_sinks, shift_mu,
           rw_w0, rw_w2, rw_a0, rw_a2, rw_g2, rw_k_k, rw_k_a, rw_r_k, rw_gn_g, rw_gn_b,
           rw_v0, rw_v1, rw_v2, w_branch_attn, w_branch_rwkv, w_out):
    batch, seq_len, _ = x.shape
    depth = w_in.shape[0]
    assert all(seq_len % n == 0 for n in (TM, TM_FFN, TM_MIX, TQ, WKV_TBLK))
    assert WKV_TBLK % MXU_DIM == 0
    row = lambda t: t.reshape(1, -1)
    bf = lambda t: t.astype(BF16)
    wgu, wd, w_proj = bf(ffn_w_gu), bf(ffn_w_down), bf(w_in)
    wba, wbb, wo = bf(w_branch_attn), bf(w_branch_rwkv), bf(w_out)
    bias = _attention_bias(rel_bias, attn_sinks)
    h = x.reshape(batch * seq_len, D_MODEL)
    v_first = None
    for l in range(depth):
        h = _ffn_ln(h, wgu, wd, (l, 0), row(ln_g[l, 0]), row(ln_b[l, 0]))
        vmix = None if l == 0 else (v_first, row(rw_v0[l - 1]), bf(rw_v1[l - 1]), bf(rw_v2[l - 1]))
        gab, q, kd, vd, r, k, v, lw, a, g = _proj_prep(
            h, seq_len, l, w_proj, row(b_in[l]), row(shift_mu[l]), row(rw_w0[l]), bf(rw_w2[l]),
            row(rw_a0[l]), bf(rw_a2[l]), bf(rw_g2[l]), vmix)
        if l == 0:
            v_first = v
        attn = _swa(q, kd, vd, bias, l, batch, seq_len)
        rw = _wkv7(r, k, v, lw, a, g, row(rw_k_k[l]), row(rw_k_a[l]), row(rw_r_k[l]),
                   row(rw_gn_g[l]), row(rw_gn_b[l]), batch, seq_len)
        h = _mix_ln(attn, rw, gab, h, l, wba, wbb, wo, row(ln_g[l, 1]), row(ln_b[l, 1]))
        h = _ffn_ln(h, wgu, wd, (l, 1), row(ln_g[l, 2]), row(ln_b[l, 2]))
    return h.reshape(batch, seq_len, D_MODEL)
```

```python
import functools
import math

import jax
import jax.numpy as jnp
from jax import lax
from jax.experimental import pallas as pl
from jax.experimental.pallas import tpu as pltpu

F32 = jnp.float32
BF16 = jnp.bfloat16

D_MODEL = 1024
DEPTH = 2
HEAD_DIM = 64
ATTN_Q_HEADS = 8
ATTN_KV_HEADS = 2
ATTN_GROUP = ATTN_Q_HEADS // ATTN_KV_HEADS
ATTN_WIDTH = ATTN_Q_HEADS * HEAD_DIM
KV_WIDTH = ATTN_KV_HEADS * HEAD_DIM
WINDOW = 128
BLOCK = 128
N_BUCKETS = 32
MAX_DISTANCE = 128
RW_HEADS = 8
RW_HEAD = 64
RW_WIDTH = RW_HEADS * RW_HEAD
DECAY_RANK = 64
A_RANK = 64
G_RANK = 128
GN_EPS = 64e-5
KK_NORM_FLOOR = 1e-12
DECAY_OFFSET = 0.5
D_FF = 2816
FFN_STEP = 0.5
ALPHA = (2 * DEPTH) ** 0.25
LN_EPS = 1e-5
RW_COLS = 3 * RW_WIDTH + DECAY_RANK + A_RANK + G_RANK

LANES = 128
SUBLANES = 8
MXU_DIM = 256
VMEM_BYTES_V7X = 64 * 1024 * 1024
VMEM_LIMIT = VMEM_BYTES_V7X * 7 // 8

TM_PROJ = 512
TM_FFN = 1024
TM_MIX = 1024
FF_CHUNK = MXU_DIM
N_FF_CHUNKS = D_FF // FF_CHUNK
TQ = 1024
CHUNK = 64
WKV_TBLK = 512
PAIR = 2 * RW_HEAD
N_PAIRS = RW_WIDTH // PAIR

PW_GATES = 2 * D_MODEL
PW_Q = ATTN_WIDTH
PW_KD = 2 * KV_WIDTH
PW_VD = 2 * KV_WIDTH
PO_Q = PW_GATES
PO_RW = PO_Q + ATTN_WIDTH + 2 * KV_WIDTH
PROJ_COLS = PO_RW + RW_COLS


def _dot(a, b):
    return jnp.dot(a, b, preferred_element_type=F32)


def _dot_nt(a, b):
    return lax.dot_general(a, b, (((1,), (1,)), ((), ())), preferred_element_type=F32)


def _layer_norm(y, g, b):
    mu = jnp.mean(y, axis=-1, keepdims=True)
    d = y - mu
    var = jnp.mean(d * d, axis=-1, keepdims=True)
    return d * lax.rsqrt(var + LN_EPS) * g + b


def _resident(shape, lead=()):
    rest = len(shape) - len(lead)
    block = (None,) * len(lead) + tuple(shape[len(lead):])
    return pl.BlockSpec(block, lambda *_: tuple(lead) + (0,) * rest, pipeline_mode=pl.Buffered(1))


def _params(*sem):
    return pltpu.CompilerParams(dimension_semantics=sem, vmem_limit_bytes=VMEM_LIMIT)


def _ffn_ln_body(x_ref, wgu_ref, wd_ref, g_ref, b_ref, o_ref):
    x = x_ref[...]
    xb = x.astype(BF16)
    acc = jnp.zeros(x.shape, F32)
    for j in range(N_FF_CHUNKS):
        cols = slice(j * FF_CHUNK, (j + 1) * FF_CHUNK)
        gate = _dot(xb, wgu_ref[:, cols])
        up = _dot(xb, wgu_ref[:, D_FF + j * FF_CHUNK:D_FF + (j + 1) * FF_CHUNK])
        h = jax.nn.silu(gate) * up
        acc = acc + _dot(h.astype(BF16), wd_ref[cols, :])
    o_ref[...] = _layer_norm(ALPHA * x + FFN_STEP * acc, g_ref[...], b_ref[...])


def _ffn_ln(x, wgu, wd, lead, g, b):
    m = x.shape[0]
    return pl.pallas_call(
        _ffn_ln_body,
        grid=(m // TM_FFN,),
        in_specs=[pl.BlockSpec((TM_FFN, D_MODEL), lambda i: (i, 0)),
                  _resident(wgu.shape, lead), _resident(wd.shape, lead),
                  _resident(g.shape), _resident(b.shape)],
        out_specs=pl.BlockSpec((TM_FFN, D_MODEL), lambda i: (i, 0)),
        out_shape=jax.ShapeDtypeStruct((m, D_MODEL), F32),
        compiler_params=_params("parallel"),
        name="ffn_ln",
    )(x, wgu, wd, g, b)


def _softplus(z):
    return jnp.maximum(z, 0.0) + jnp.log1p(jnp.exp(-jnp.abs(z)))


def _proj_body(tiles_per_seq, has_vmix, *refs):
    (x_ref, w_ref, b_ref, mu_ref, w0_ref, w2_ref, a0_ref, a2_ref, g2_ref) = refs[:9]
    refs = refs[9:]
    if has_vmix:
        vf_ref, v0_ref, v1_ref, v2_ref = refs[:4]
        refs = refs[4:]
    (gab_ref, q_ref, kd_ref, vd_ref, r_ref, k_ref, v_ref, lw_ref, a_ref, g_ref, carry_ref) = refs

    i = pl.program_id(0)
    tm = x_ref.shape[0]

    @pl.when(i % tiles_per_seq == 0)
    def _():
        carry_ref[...] = jnp.zeros(carry_ref.shape, F32)

    xb = x_ref[...].astype(BF16)
    part = lambda lo, hi: _dot(xb, w_ref[:, lo:hi]) + b_ref[:, lo:hi]
    u = part(PO_RW, PROJ_COLS)
    gates = part(0, PW_GATES)

    before_tile = carry_ref[SUBLANES - 1:SUBLANES, :]
    carry_ref[...] = u[tm - SUBLANES:, :]
    row = lax.broadcasted_iota(jnp.int32, (tm, 1), 0)
    u_prev = jnp.where(row == 0, before_tile, pltpu.roll(u, 1, 0))
    u = u + (u_prev - u) * mu_ref[...]

    c = RW_WIDTH
    r = u[:, :c]
    k = u[:, c:2 * c]
    v = u[:, 2 * c:3 * c]
    o = 3 * c
    xw = u[:, o:o + DECAY_RANK]
    xa = u[:, o + DECAY_RANK:o + DECAY_RANK + A_RANK]
    xg = u[:, o + DECAY_RANK + A_RANK:]

    w = -_softplus(-(w0_ref[...] + _dot(jnp.tanh(xw).astype(BF16), w2_ref[...]))) - DECAY_OFFSET
    lw_ref[...] = -jnp.exp(w)
    a_ref[...] = jax.nn.sigmoid(a0_ref[...] + _dot(xa.astype(BF16), a2_ref[...]))
    g_ref[...] = _dot(jax.nn.sigmoid(xg).astype(BF16), g2_ref[...]).astype(BF16)
    if has_vmix:
        low = _dot(v.astype(BF16), v1_ref[...])
        mix = jax.nn.sigmoid(v0_ref[...] + _dot(low.astype(BF16), v2_ref[...]))
        v = v + (vf_ref[...] - v) * mix
    r_ref[...] = r
    k_ref[...] = k
    v_ref[...] = v

    qkv = part(PO_Q, PO_RW)
    gab_ref[...] = jax.nn.sigmoid(gates).astype(BF16)
    q_ref[...] = (qkv[:, :ATTN_WIDTH] * (HEAD_DIM ** -0.5)).astype(BF16)
    lo = lax.broadcasted_iota(jnp.int32, (1, LANES), 1) < HEAD_DIM
    for ref, at in ((kd_ref, ATTN_WIDTH), (vd_ref, ATTN_WIDTH + KV_WIDTH)):
        pair = qkv[:, at:at + KV_WIDTH]
        swapped = pltpu.roll(pair, HEAD_DIM, 1)
        ref[...] = jnp.concatenate([jnp.where(lo, pair, swapped),
                                    jnp.where(lo, swapped, pair)], axis=1).astype(BF16)


def _proj_prep(x, seq_len, layer, w, b, mu, w0, w2, a0, a2, g2, vmix):
    m = x.shape[0]
    tile = lambda n: pl.BlockSpec((TM_PROJ, n), lambda i: (i, 0))
    has_vmix = vmix is not None
    args = [x, w, b, mu, w0, w2, a0, a2, g2]
    in_specs = [tile(D_MODEL), _resident(w.shape, (layer,))] + [_resident(t.shape) for t in args[2:]]
    if has_vmix:
        vf, v0, v1, v2 = vmix
        args += [vf, v0, v1, v2]
        in_specs += [tile(RW_WIDTH), _resident(v0.shape), _resident(v1.shape), _resident(v2.shape)]
    widths = [(PW_GATES, BF16), (PW_Q, BF16), (PW_KD, BF16), (PW_VD, BF16)] \
        + [(RW_WIDTH, F32)] * 5 + [(RW_WIDTH, BF16)]
    return pl.pallas_call(
        functools.partial(_proj_body, seq_len // TM_PROJ, has_vmix),
        grid=(m // TM_PROJ,),
        in_specs=in_specs,
        out_specs=[tile(n) for n, _ in widths],
        out_shape=[jax.ShapeDtypeStruct((m, n), dt) for n, dt in widths],
        scratch_shapes=[pltpu.VMEM((SUBLANES, RW_COLS), F32)],
        compiler_params=_params("arbitrary"),
        name="proj_prep",
    )(*args)


def _swa_body(q_ref, kc_ref, kp_ref, vc_ref, vp_ref, bias_ref, o_ref):
    i = pl.program_id(1)
    lane = lax.broadcasted_iota(jnp.int32, (1, LANES), 1)
    lo = lane < HEAD_DIM
    key = lax.broadcasted_iota(jnp.int32, (1, 2 * BLOCK), 1)
    not_key0 = lax.broadcasted_iota(jnp.int32, (2 * BLOCK, 1), 0) > 0
    no_prev = jnp.logical_and(i == 0, jnp.logical_and(key < BLOCK, key > 0))
    ones = jnp.ones((2 * BLOCK, LANES), BF16)
    zero = jnp.zeros((), BF16)
    units = [(j, h) for j in range(TQ // BLOCK) for h in range(ATTN_KV_HEADS)]
    idx = range(len(units))
    scores, v_lo, v_hi = [], [], []
    for j, h in units:
        rows = slice(j * BLOCK, (j + 1) * BLOCK)
        cols = slice(h * LANES, (h + 1) * LANES)
        if j == 0:
            k_prev, v_prev = kp_ref[:, cols], vp_ref[:, cols]
        else:
            before = slice((j - 1) * BLOCK, j * BLOCK)
            k_prev, v_prev = kc_ref[before, cols], vc_ref[before, cols]
        kb = jnp.concatenate([k_prev, kc_ref[rows, cols]], axis=0)
        vb = jnp.concatenate([v_prev, vc_ref[rows, cols]], axis=0)
        kb, vb = jnp.where(not_key0, kb, zero), jnp.where(not_key0, vb, zero)
        parts = []
        for p in (2 * h, 2 * h + 1):
            qp = q_ref[rows, p * LANES:(p + 1) * LANES]
            parts += [jnp.where(lo, qp, zero), jnp.where(lo, zero, qp)]
        s = _dot_nt(jnp.concatenate(parts, axis=0), kb) + bias_ref[h]
        if j == 0:
            s = jnp.where(no_prev, -jnp.inf, s)
        scores.append(s)
        v_lo.append(jnp.where(lo, vb, zero))
        v_hi.append(jnp.where(lo, zero, vb))
    mx = [jnp.max(scores[u], axis=-1, keepdims=True) for u in idx]
    eb = [jnp.exp(scores[u] - mx[u]).astype(BF16) for u in idx]
    den = [_dot(eb[u], ones) for u in idx]
    for u, (j, h) in enumerate(units):
        rows = slice(j * BLOCK, (j + 1) * BLOCK)
        for n, p in enumerate((2 * h, 2 * h + 1)):
            top = slice(2 * n * BLOCK, (2 * n + 1) * BLOCK)
            bot = slice((2 * n + 1) * BLOCK, (2 * n + 2) * BLOCK)
            out = (_dot(eb[u][top], v_lo[u]) / den[u][top]
                   + _dot(eb[u][bot], v_hi[u]) / den[u][bot])
            o_ref[rows, p * LANES:(p + 1) * LANES] = out.astype(BF16)


def _swa(q, kd, vd, bias, layer, batch, seq_len):
    q3 = q.reshape(batch, seq_len, PW_Q)
    kd3 = kd.reshape(batch, seq_len, PW_KD)
    vd3 = vd.reshape(batch, seq_len, PW_VD)
    cur = lambda n: pl.BlockSpec((None, TQ, n), lambda b, i: (b, i, 0))
    prev = lambda n: pl.BlockSpec(
        (None, BLOCK, n), lambda b, i: (b, jnp.maximum(i * (TQ // BLOCK) - 1, 0), 0))
    out = pl.pallas_call(
        _swa_body,
        grid=(batch, seq_len // TQ),
        in_specs=[cur(PW_Q), cur(PW_KD), prev(PW_KD), cur(PW_VD), prev(PW_VD),
                  _resident(bias.shape, (layer,))],
        out_specs=cur(ATTN_WIDTH),
        out_shape=jax.ShapeDtypeStruct((batch, seq_len, ATTN_WIDTH), BF16),
        compiler_params=_params("parallel", "parallel"),
        name="swa",
    )(q3, kd3, kd3, vd3, vd3, bias)
    return out.reshape(batch * seq_len, ATTN_WIDTH)


def _split2(x):
    hi = x.astype(BF16)
    return hi, (x - hi.astype(F32)).astype(BF16)


def _wkv_body(r_ref, k_ref, v_ref, lw_ref, a_ref, g_ref,
              kk_ref, ka_ref, rk_ref, gng_ref, gnb_ref, o_ref, s_ref):
    t = pl.program_id(1)
    tblk, width = r_ref.shape
    n_pairs = width // PAIR

    @pl.when(t == 0)
    def _():
        s_ref[...] = jnp.zeros(s_ref.shape, F32)

    lane = lax.broadcasted_iota(jnp.int32, (1, PAIR), 1)
    lo = lane < RW_HEAD
    ri = lax.broadcasted_iota(jnp.int32, (PAIR, PAIR), 0)
    ci = lax.broadcasted_iota(jnp.int32, (PAIR, PAIR), 1)
    same_head = (ri >= RW_HEAD) == (ci >= RW_HEAD)
    head_ones = jnp.where(same_head, 1.0, 0.0).astype(BF16)
    eye = ri == ci
    tok_r = lax.broadcasted_iota(jnp.int32, (CHUNK, PAIR), 0)
    tok_c = lax.broadcasted_iota(jnp.int32, (CHUNK, PAIR), 1) & (CHUNK - 1)
    strict, incl, diag = tok_c < tok_r, tok_c <= tok_r, tok_c == tok_r
    rt_ = lax.broadcasted_iota(jnp.int32, (MXU_DIM, MXU_DIM), 0)
    ct_ = lax.broadcasted_iota(jnp.int32, (MXU_DIM, MXU_DIM), 1)
    same_chunk = (rt_ & -CHUNK) == (ct_ & -CHUNK)
    chunk_tril = jnp.where(jnp.logical_and(same_chunk, ct_ <= rt_), 1.0, 0.0).astype(BF16)

    def seg_sum(x):
        xb = x.astype(BF16)
        return jnp.concatenate(
            [_dot(xb[:, p * PAIR:(p + 1) * PAIR], head_ones) for p in range(n_pairs)], axis=1)

    def stack(x):
        x = x.astype(BF16)
        zero = jnp.zeros((), BF16)
        return jnp.concatenate([jnp.where(lo, x, zero), jnp.where(lo, zero, x)], axis=0)

    def per_head(x):
        return jnp.where(jnp.concatenate([same_head] * (x.shape[1] // PAIR), axis=1), x, 0.0)

    r, k, v, lw, a = r_ref[...], k_ref[...], v_ref[...], lw_ref[...], a_ref[...]
    kk = k * kk_ref[...]
    kk = kk / jnp.maximum(jnp.sqrt(seg_sum(kk * kk)), KK_NORM_FLOOR)
    k = k * (1.0 + (a - 1.0) * ka_ref[...])
    b = kk * a
    bonus = seg_sum(r * k * rk_ref[...]) * v

    lw_hi, lw_lo = _split2(lw)
    cum = jnp.concatenate(
        [_dot(chunk_tril, lw_hi[i:i + MXU_DIM]) + _dot(chunk_tril, lw_lo[i:i + MXU_DIM])
         for i in range(0, tblk, MXU_DIM)], axis=0)

    n_chunks = tblk // CHUNK
    units = [(p, c) for c in range(n_chunks) for p in range(n_pairs)]
    idx = range(len(units))
    n_levels = int(math.log2(CHUNK))
    at, rt, lhs, rhs, bh, kh, vd, vs, dec = [], [], [], [], [], [], [], [], []
    for p, c in units:
        rows, cols = slice(c * CHUNK, (c + 1) * CHUNK), slice(p * PAIR, (p + 1) * PAIR)
        cu, lwu = cum[rows, cols], lw[rows, cols]
        ru, ku, bu, vu = r[rows, cols], k[rows, cols], b[rows, cols], v[rows, cols]
        end = cu[CHUNK - 1:CHUNK, :]
        e_pos, e_neg = jnp.exp(cu), jnp.exp(-cu)
        e_prev, e_end = jnp.exp(cu - lwu), jnp.exp(end - cu)
        at.append(-kk[rows, cols] * e_prev)
        rt.append(ru * e_pos)
        lhs.append(jnp.concatenate([at[-1], rt[-1]], axis=0).astype(BF16))
        rhs.append(jnp.concatenate([stack(bu * e_neg), stack(ku * e_neg)], axis=0))
        bh.append(jnp.transpose(bu * e_end).astype(BF16))
        kh.append(jnp.transpose(ku * e_end).astype(BF16))
        vd.append(vu.astype(BF16))
        vs.append(stack(vu))
        dec.append(jnp.where(eye, jnp.exp(end), 0.0))

    g = [_dot_nt(lhs[u], rhs[u]) for u in idx]
    a_ab = [jnp.where(strict, g[u][:CHUNK, :PAIR], 0.0) for u in idx]
    a_rb = [jnp.where(incl, g[u][CHUNK:, :PAIR], 0.0).astype(BF16) for u in idx]
    a_kv = [jnp.concatenate([jnp.where(strict, g[u][:CHUNK, PAIR:], 0.0),
                             jnp.where(incl, g[u][CHUNK:, PAIR:], 0.0)], axis=0).astype(BF16) for u in idx]
    akv = [_dot(a_kv[u], vs[u]) for u in idx]
    khv = [_dot(kh[u], vd[u]) for u in idx]

    inv = [jnp.where(diag, 1.0, a_ab[u]) for u in idx]
    pw = [a_ab[u].astype(BF16) for u in idx]
    pw = [_dot(pw[u], stack(pw[u])).astype(BF16) for u in idx]
    for level in range(1, n_levels):
        if level + 1 < n_levels:
            both = [_dot(pw[u], jnp.concatenate([stack(inv[u]), stack(pw[u])], axis=1)) for u in idx]
            inv = [inv[u] + both[u][:, :PAIR] for u in idx]
            pw = [both[u][:, PAIR:].astype(BF16) for u in idx]
        else:
            inv = [inv[u] + _dot(pw[u], stack(inv[u])) for u in idx]
    x = [_dot(inv[u].astype(BF16),
              jnp.concatenate([stack(at[u]), stack(akv[u][:CHUNK])], axis=1)).astype(BF16) for u in idx]

    qy = [_dot(a_rb[u], jnp.concatenate([stack(x[u][:, :PAIR]), stack(x[u][:, PAIR:])], axis=1))
          + jnp.concatenate([rt[u], akv[u][CHUNK:]], axis=1) for u in idx]
    mc = [per_head(_dot(bh[u], x[u])) + jnp.concatenate([dec[u], per_head(khv[u])], axis=1)
          for u in idx]
    q_hat = [qy[u][:, :PAIR].astype(BF16) for u in idx]
    m_hat = [mc[u][:, :PAIR].astype(BF16) for u in idx]

    state = [s_ref[p] for p in range(n_pairs)]
    ys = [[None] * n_chunks for _ in range(n_pairs)]
    for u, (p, c) in enumerate(units):
        sb = state[p].astype(BF16)
        ys[p][c] = _dot(q_hat[u], sb) + qy[u][:, PAIR:]
        state[p] = _dot(m_hat[u], sb) + mc[u][:, PAIR:]
    for p in range(n_pairs):
        s_ref[p] = state[p]

    y = jnp.concatenate([jnp.concatenate(ys[p], axis=0) for p in range(n_pairs)], axis=1)
    inv_n = 1.0 / RW_HEAD
    d = y - seg_sum(y) * inv_n
    var = seg_sum(d * d) * inv_n
    y = d * lax.rsqrt(var + GN_EPS) * gng_ref[...] + gnb_ref[...]
    o_ref[...] = ((y + bonus) * g_ref[...].astype(F32)).astype(BF16)


def _wkv7(r, k, v, lw, a, g, k_k, k_a, r_k, gn_g, gn_b, batch, seq_len):
    tok = pl.BlockSpec((None, WKV_TBLK, RW_WIDTH), lambda b, t: (b, t, 0))
    shape3 = (batch, seq_len, RW_WIDTH)
    params = (k_k, k_a, r_k, gn_g, gn_b)
    out = pl.pallas_call(
        _wkv_body,
        grid=(batch, seq_len // WKV_TBLK),
        in_specs=[tok] * 6 + [_resident(t.shape) for t in params],
        out_specs=tok,
        out_shape=jax.ShapeDtypeStruct(shape3, BF16),
        scratch_shapes=[pltpu.VMEM((N_PAIRS, PAIR, PAIR), F32)],
        compiler_params=_params("parallel", "arbitrary"),
        name="wkv7",
    )(*(t.reshape(shape3) for t in (r, k, v, lw, a, g)), *params)
    return out.reshape(batch * seq_len, RW_WIDTH)


def _mix_ln_body(attn_ref, rw_ref, gab_ref, x_ref, wba_ref, wbb_ref, wo_ref, g_ref, b_ref, o_ref):
    half = x_ref.shape[0] // 2
    for part in range(2):
        rows = slice(part * half, (part + 1) * half)
        gab = gab_ref[rows, :]
        merged = (gab[:, :D_MODEL].astype(F32) * _dot(attn_ref[rows, :], wba_ref[...])
                  + gab[:, D_MODEL:].astype(F32) * _dot(rw_ref[rows, :], wbb_ref[...]))
        mix = _dot(merged.astype(BF16), wo_ref[...])
        o_ref[rows, :] = _layer_norm(ALPHA * x_ref[rows, :] + mix, g_ref[...], b_ref[...])


def _mix_ln(attn, rw, gab, x, layer, wba, wbb, wo, g, b):
    m = x.shape[0]
    tile = lambda n: pl.BlockSpec((TM_MIX, n), lambda i: (i, 0))
    return pl.pallas_call(
        _mix_ln_body,
        grid=(m // TM_MIX,),
        in_specs=[tile(ATTN_WIDTH), tile(RW_WIDTH), tile(PW_GATES), tile(D_MODEL)]
        + [_resident(t.shape, (layer,)) for t in (wba, wbb, wo)] + [_resident(g.shape), _resident(b.shape)],
        out_specs=tile(D_MODEL),
        out_shape=jax.ShapeDtypeStruct((m, D_MODEL), F32),
        compiler_params=_params("parallel"),
        name="mix_ln",
    )(attn, rw, gab, x, wba, wbb, wo, g, b)


def _t5_bucket(n):
    max_exact = N_BUCKETS // 2
    nf = jnp.maximum(n, 1).astype(F32)
    large = max_exact + (jnp.log(nf / max_exact) / math.log(MAX_DISTANCE / max_exact)
                         * (N_BUCKETS - max_exact)).astype(jnp.int32)
    large = jnp.minimum(large, N_BUCKETS - 1)
    return jnp.where(n < max_exact, n, large)


def _attention_bias(rel_bias, sinks):
    dist_bias = rel_bias[_t5_bucket(jnp.arange(WINDOW, dtype=jnp.int32))].T
    period = 3 * BLOCK
    neg = lambda n: jnp.full((ATTN_Q_HEADS, n), -jnp.inf, F32)
    row0 = jnp.concatenate([neg(1), dist_bias[:, ::-1].astype(F32), neg(period - 1 - WINDOW)], axis=1)
    bias = jnp.tile(row0, (1, BLOCK))[:, :BLOCK * (period - 1)]
    bias = bias.reshape(ATTN_Q_HEADS, BLOCK, period - 1)[:, :, :2 * BLOCK]
    key = jnp.arange(2 * BLOCK)
    bias = jnp.where(key == 0, sinks.astype(F32)[:, :, None, None], bias[None])
    return bias.reshape(sinks.shape[0], ATTN_KV_HEADS, ATTN_GROUP * BLOCK, 2 * BLOCK)


def kernel(x, ffn_w_gu, ffn_w_down, ln_g, ln_b, w_in, b_in, rel_bias, attn_sinks, shift_mu,
           rw_w0, rw_w2, rw_a0, rw_a2, rw_g2, rw_k_k, rw_k_a, rw_r_k, rw_gn_g, rw_gn_b,
           rw_v0, rw_v1, rw_v2, w_branch_attn, w_branch_rwkv, w_out):
    batch, seq_len, _ = x.shape
    depth = w_in.shape[0]
    assert all(seq_len % n == 0 for n in (TM_PROJ, TM_FFN, TM_MIX, TQ, WKV_TBLK))
    assert WKV_TBLK % MXU_DIM == 0
    row = lambda t: t.reshape(1, -1)
    bf = lambda t: t.astype(BF16)
    wgu, wd, w_proj = bf(ffn_w_gu), bf(ffn_w_down), bf(w_in)
    wba, wbb, wo = bf(w_branch_attn), bf(w_branch_rwkv), bf(w_out)
    bias = _attention_bias(rel_bias, attn_sinks)
    h = x.reshape(batch * seq_len, D_MODEL)
    v_first = None
    for l in range(depth):
        h = _ffn_ln(h, wgu, wd, (l, 0), row(ln_g[l, 0]), row(ln_b[l, 0]))
        vmix = None if l == 0 else (v_first, row(rw_v0[l - 1]), bf(rw_v1[l - 1]), bf(rw_v2[l - 1]))
        gab, q, kd, vd, r, k, v, lw, a, g = _proj_prep(
            h, seq_len, l, w_proj, row(b_in[l]), row(shift_mu[l]), row(rw_w0[l]), bf(rw_w2[l]),
            row(rw_a0[l]), bf(rw_a2[l]), bf(rw_g2[l]), vmix)
        if l == 0:
            v_first = v
        attn = _swa(q, kd, vd, bias, l, batch, seq_len)
        rw = _wkv7(r, k, v, lw, a, g, row(rw_k_k[l]), row(rw_k_a[l]), row(rw_r_k[l]),
                   row(rw_gn_g[l]), row(rw_gn_b[l]), batch, seq_len)
        h = _mix_ln(attn, rw, gab, h, l, wba, wbb, wo, row(ln_g[l, 1]), row(ln_b[l, 1]))
        h = _ffn_ln(h, wgu, wd, (l, 1), row(ln_g[l, 2]), row(ln_b[l, 2]))
    return h.reshape(batch, seq_len, D_MODEL)
```
